```python
import jax, jax.numpy as jnp
from jax import lax
import numpy as np

D_MODEL = 1024
BATCH = 32
SEQ = 2048
DEPTH = 2

CTX_LEN = 256
GRID_W = 64

D_LRU = 512
LRU_BLOCKS = 8
LRU_BLOCK = D_LRU // LRU_BLOCKS
LRU_CONV = 4
LRU_CONV_LEFT = 2
LRU_C = 8.0
D_FOURIER = 256
D_POOL = 256
POOL_WINDOWS = (2, 4, 8, 16)
POOL_GROUP = D_POOL // len(POOL_WINDOWS)
D_SCONV = 256
SCONV_WIDTH = 3
SCONV_LEFT = 1
N_BRANCH = 4
D_FF = 4 * D_MODEL
N_MOD = 6
EPS = 1e-6

OFF_LRU_X = 0
OFF_LRU_G = OFF_LRU_X + D_LRU
OFF_FOURIER = OFF_LRU_G + D_LRU
OFF_POOL = OFF_FOURIER + D_FOURIER
OFF_SCONV = OFF_POOL + D_POOL
OFF_GATE = OFF_SCONV + 3 * D_SCONV
N_IN = OFF_GATE + N_BRANCH * D_MODEL

kernel_name = "hybrid_lru_fourier_pool_conv_dit"


def rmsnorm(x, g):
    xf = x.astype(jnp.float32)
    y = xf * lax.rsqrt(jnp.mean(xf * xf, axis=-1, keepdims=True) + EPS)
    return (y * g.astype(jnp.float32)).astype(x.dtype)


def modulate(u, shift, scale):
    return u * (1.0 + scale) + shift


def depthwise_conv(x, w, left):
    k, ch = w.shape
    return lax.conv_general_dilated(
        x, w[:, None, :].astype(x.dtype), window_strides=(1,),
        padding=[(left, k - 1 - left)],
        dimension_numbers=("NWC", "WIO", "NWC"), feature_group_count=ch)


def rglru_bidir(xa, w_a, b_a, w_x, b_x, lam, h0):
    bn, length, _ = xa.shape
    xf = xa.astype(jnp.float32)
    xs = jnp.stack([xf, xf[:, ::-1]], axis=0)
    xh = xs.reshape(2, bn, length, LRU_BLOCKS, LRU_BLOCK)
    r = jax.nn.sigmoid(jnp.einsum('dblhi,dhij->dblhj', xh, w_a.astype(jnp.float32))
                       .reshape(2, bn, length, D_LRU) + b_a.astype(jnp.float32)[:, None, None])
    gi = jax.nn.sigmoid(jnp.einsum('dblhi,dhij->dblhj', xh, w_x.astype(jnp.float32))
                        .reshape(2, bn, length, D_LRU) + b_x.astype(jnp.float32)[:, None, None])
    log_a = -LRU_C * r * jax.nn.softplus(-lam.astype(jnp.float32))[:, None, None]
    a = jnp.exp(log_a)
    b = jnp.sqrt(-jnp.expm1(2.0 * log_a)) * (gi * xs)
    b = b.at[:, :, 0].add(a[:, :, 0] * h0)

    def combine(e1, e2):
        a1, b1 = e1
        a2, b2 = e2
        return a1 * a2, a2 * b1 + b2

    _, h = lax.associative_scan(combine, (a, b), axis=2)
    return h


def fourier_mix(u):
    return jnp.fft.fft2(u.astype(jnp.float32), axes=(1, 2), norm="ortho").real.astype(u.dtype)


def multiscale_pool(u, pool_w, pool_scale):
    length = u.shape[2]
    uf = u.astype(jnp.float32)
    cs = jnp.pad(jnp.cumsum(uf, axis=2), ((0, 0), (0, 0), (1, 0), (0, 0)))
    t = jnp.arange(length)
    outs = []
    for gidx, w in enumerate(POOL_WINDOWS):
        lo = jnp.maximum(t - w // 2, 0)
        hi = jnp.minimum(t + w // 2, length)
        sl = slice(gidx * POOL_GROUP, (gidx + 1) * POOL_GROUP)
        csg = cs[..., sl]
        mean = (jnp.take(csg, hi, axis=2) - jnp.take(csg, lo, axis=2)) / (hi - lo).astype(jnp.float32)[:, None]
        outs.append(mean - uf[..., sl])
    p = jnp.concatenate(outs, axis=-1)
    shp = p.shape
    p = jnp.einsum('bnlgi,gij->bnlgj', p.reshape(shp[:-1] + (len(POOL_WINDOWS), POOL_GROUP)),
                   pool_w.astype(jnp.float32)).reshape(shp)
    return (p * pool_scale.astype(jnp.float32)).astype(u.dtype)


def token_mixer(u, pool_rows, h0, w_in, conv_w, conv_b, w_a, b_a, w_x, b_x, lam,
                pool_w, pool_scale, sconv_w, w_br_lru, w_br_fourier, w_br_pool,
                w_br_sconv, w_out):
    bn, length, _ = u.shape
    z = u @ w_in
    xa = depthwise_conv(z[..., OFF_LRU_X:OFF_LRU_G], conv_w, LRU_CONV_LEFT) + conv_b
    h = rglru_bidir(xa, w_a, b_a, w_x, b_x, lam, h0)
    h_last = h[:, :, -1]
    y_lru = ((h[0] + h[1][:, ::-1]) * jax.nn.gelu(z[..., OFF_LRU_G:OFF_FOURIER].astype(jnp.float32))).astype(u.dtype)
    y_fourier = fourier_mix(z[..., OFF_FOURIER:OFF_POOL])
    zp = z[..., OFF_POOL:OFF_SCONV].reshape(bn, pool_rows, length // pool_rows, D_POOL)
    y_pool = multiscale_pool(zp, pool_w, pool_scale).reshape(bn, length, D_POOL)
    zs = z[..., OFF_SCONV:OFF_GATE]
    gb, gc, hs = zs[..., :D_SCONV], zs[..., D_SCONV:2 * D_SCONV], zs[..., 2 * D_SCONV:]
    y_sconv = gb * depthwise_conv(gc * hs, sconv_w, SCONV_LEFT)
    gates = jax.nn.sigmoid(z[..., OFF_GATE:].reshape(bn, length, N_BRANCH, D_MODEL))
    merged = (gates[:, :, 0] * (y_lru @ w_br_lru)
              + gates[:, :, 1] * (y_fourier @ w_br_fourier)
              + gates[:, :, 2] * (y_pool @ w_br_pool)
              + gates[:, :, 3] * (y_sconv @ w_br_sconv))
    return merged @ w_out, h_last


def sqrelu_mlp(u, w1, w2):
    return jnp.square(jax.nn.relu(u @ w1)) @ w2


def setup_inputs(seed: int = 0) -> dict:
    key = jax.random.key(seed)
    ks = jax.random.split(key, 32)

    def nrm(k, shape, scale):
        return jax.random.normal(k, shape, jnp.float32) * scale

    u_lam = jax.random.uniform(ks[14], (DEPTH, 2, D_LRU), jnp.float32, 0.9, 0.999)
    a_lam = u_lam ** (1.0 / LRU_C)
    return {
        "x": nrm(ks[0], (BATCH, SEQ, D_MODEL), 1.0),
        "c": nrm(ks[1], (BATCH, D_MODEL), 1.0),
        "ctx": nrm(ks[2], (BATCH, CTX_LEN, D_MODEL), 1.0),
        "c_ctx": nrm(ks[3], (D_MODEL,), 1.0),
        "w_mod": nrm(ks[4], (DEPTH, D_MODEL, N_MOD * D_MODEL), 0.5 * D_MODEL ** -0.5),
        "b_mod": nrm(ks[5], (DEPTH, N_MOD * D_MODEL), 0.02),
        "g_norm1": 1.0 + nrm(ks[6], (DEPTH, D_MODEL), 0.02),
        "g_norm2": 1.0 + nrm(ks[7], (DEPTH, D_MODEL), 0.02),
        "w_in": nrm(ks[8], (DEPTH, D_MODEL, N_IN), D_MODEL ** -0.5),
        "lru_conv_w": nrm(ks[9], (DEPTH, LRU_CONV, D_LRU), LRU_CONV ** -0.5),
        "lru_conv_b": nrm(ks[10], (DEPTH, D_LRU), 0.02),
        "lru_w_a": nrm(ks[11], (DEPTH, 2, LRU_BLOCKS, LRU_BLOCK, LRU_BLOCK), LRU_BLOCK ** -0.5),
        "lru_b_a": nrm(ks[12], (DEPTH, 2, D_LRU), 0.1),
        "lru_w_x": nrm(ks[13], (DEPTH, 2, LRU_BLOCKS, LRU_BLOCK, LRU_BLOCK), LRU_BLOCK ** -0.5),
        "lru_b_x": nrm(ks[15], (DEPTH, 2, D_LRU), 0.1),
        "lru_lam": jnp.log(a_lam) - jnp.log1p(-a_lam),
        "pool_w": nrm(ks[16], (DEPTH, len(POOL_WINDOWS), POOL_GROUP, POOL_GROUP), POOL_GROUP ** -0.5),
        "pool_scale": 1.0 + nrm(ks[17], (DEPTH, D_POOL), 0.1),
        "sconv_w": nrm(ks[18], (DEPTH, SCONV_WIDTH, D_SCONV), SCONV_WIDTH ** -0.5),
        "w_br_lru": nrm(ks[19], (DEPTH, D_LRU, D_MODEL), D_LRU ** -0.5),
        "w_br_fourier": nrm(ks[20], (DEPTH, D_FOURIER, D_MODEL), D_FOURIER ** -0.5),
        "w_br_pool": nrm(ks[21], (DEPTH, D_POOL, D_MODEL), D_POOL ** -0.5),
        "w_br_sconv": nrm(ks[22], (DEPTH, D_SCONV, D_MODEL), D_SCONV ** -0.5),
        "w_out": nrm(ks[23], (DEPTH, D_MODEL, D_MODEL), D_MODEL ** -0.5),
        "w_ff1": nrm(ks[24], (DEPTH, D_MODEL, D_FF), D_MODEL ** -0.5),
        "w_ff2": nrm(ks[25], (DEPTH, D_FF, D_MODEL), D_FF ** -0.5),
        "g_final": 1.0 + nrm(ks[26], (D_MODEL,), 0.02),
    }


def reference(x, c, ctx, c_ctx, w_mod, b_mod, g_norm1, g_norm2, w_in, lru_conv_w, lru_conv_b,
              lru_w_a, lru_b_a, lru_w_x, lru_b_x, lru_lam, pool_w, pool_scale, sconv_w,
              w_br_lru, w_br_fourier, w_br_pool, w_br_sconv, w_out, w_ff1, w_ff2, g_final):
    bn = x.shape[0]
    rows = x.shape[1] // GRID_W
    for l in range(DEPTH):
        mixer_params = (w_in[l], lru_conv_w[l], lru_conv_b[l], lru_w_a[l], lru_b_a[l],
                        lru_w_x[l], lru_b_x[l], lru_lam[l], pool_w[l], pool_scale[l], sconv_w[l],
                        w_br_lru[l], w_br_fourier[l], w_br_pool[l], w_br_sconv[l], w_out[l])
        mod = (jax.nn.silu(c) @ w_mod[l] + b_mod[l]).reshape(bn, N_MOD, 1, D_MODEL)
        mod_c = (jax.nn.silu(c_ctx) @ w_mod[l] + b_mod[l]).reshape(N_MOD, D_MODEL)
        uc = modulate(rmsnorm(ctx, g_norm1[l]), mod_c[0], mod_c[1])
        h_zero = jnp.zeros((2, bn, D_LRU), jnp.float32)
        if l < DEPTH - 1:
            yc, h_ctx = token_mixer(uc, 1, h_zero, *mixer_params)
            ctx = ctx + mod_c[2] * yc
            ctx = ctx + mod_c[5] * sqrelu_mlp(
                modulate(rmsnorm(ctx, g_norm2[l]), mod_c[3], mod_c[4]), w_ff1[l], w_ff2[l])
        else:
            xa_c = depthwise_conv(uc @ w_in[l][:, OFF_LRU_X:OFF_LRU_G], lru_conv_w[l], LRU_CONV_LEFT) + lru_conv_b[l]
            h_ctx = rglru_bidir(xa_c, lru_w_a[l], lru_b_a[l], lru_w_x[l], lru_b_x[l], lru_lam[l], h_zero)[:, :, -1]
        ux = modulate(rmsnorm(x, g_norm1[l]), mod[:, 0], mod[:, 1])
        yx, _ = token_mixer(ux, rows, h_ctx, *mixer_params)
        x = x + mod[:, 2] * yx
        x = x + mod[:, 5] * sqrelu_mlp(
            modulate(rmsnorm(x, g_norm2[l]), mod[:, 3], mod[:, 4]), w_ff1[l], w_ff2[l])
    return rmsnorm(x, g_final)
```

```python
import functools

import numpy as np
import jax
import jax.numpy as jnp
from jax import lax
from jax.experimental import pallas as pl
from jax.experimental.pallas import tpu as pltpu

F32 = jnp.float32
BF16 = jnp.bfloat16

EPS = 1e-6
LRU_C = 8.0
LRU_BLOCKS = 8
LRU_CONV = 4
LRU_CONV_LEFT = 2
POOL_WINDOWS = (2, 4, 8, 16)
SCONV_WIDTH = 3
N_BRANCH = 4
N_MOD = 6
GRID_W = 64

V7X_LANES = 128
V7X_SUBLANES = 8
V7X_VMEM_LIMIT_BYTES = 56 * 1024 * 1024

POOL_TILE = 256
LRU_TILE = 256
LRU_BATCH = V7X_SUBLANES


def _params(sem, vmem=V7X_VMEM_LIMIT_BYTES):
    return pltpu.CompilerParams(dimension_semantics=sem, vmem_limit_bytes=vmem)


def _const_spec(shape):
    nd = len(shape)
    return pl.BlockSpec(shape, lambda *_: (0,) * nd, pipeline_mode=pl.Buffered(1))


def _norm_mod(x, g, shift, scale):
    ms = jnp.mean(x * x, axis=-1, keepdims=True)
    y = (x * lax.rsqrt(ms + EPS)) * g
    return y * (1.0 + scale) + shift


def _dot(a, b):
    return jnp.dot(a, b, preferred_element_type=F32)


def _mod_body(c_ref, w_ref, b_ref, o_ref):
    c = c_ref[...]
    s = c * jax.nn.sigmoid(c)
    o_ref[0] = _dot(s.astype(BF16), w_ref[0].astype(BF16)) + b_ref[0]


def _mod_call(c_all, w_mod, b_mod):
    depth, d, n = w_mod.shape
    rows = c_all.shape[0]
    tn = n // 4
    return pl.pallas_call(
        _mod_body,
        grid=(depth, n // tn),
        in_specs=[
            pl.BlockSpec((rows, d), lambda l, j: (0, 0)),
            pl.BlockSpec((1, d, tn), lambda l, j: (l, 0, j)),
            pl.BlockSpec((1, 1, tn), lambda l, j: (l, 0, j)),
        ],
        out_specs=pl.BlockSpec((1, rows, tn), lambda l, j: (l, 0, j)),
        out_shape=jax.ShapeDtypeStruct((depth, rows, n), F32),
        compiler_params=_params(("parallel", "parallel")),
        name="adaln_mod",
    )(c_all, w_mod, b_mod.reshape(depth, 1, n))


def _inproj_body(widths, x_ref, sh_ref, sc_ref, g_ref, w_ref, *o_refs):
    u = _norm_mod(x_ref[0], g_ref[...], sh_ref[0], sc_ref[0]).astype(BF16)
    off = 0
    for o_ref, wd in zip(o_refs, widths):
        o_ref[0] = _dot(u, w_ref[:, off:off + wd]).astype(o_ref.dtype)
        off += wd


def _inproj_call(x, shift, scale, g, w, widths, dtypes):
    b, l, d = x.shape
    tl = min(l, 512)
    vec = pl.BlockSpec((1, 1, d), lambda i, j: (i, 0, 0))
    return pl.pallas_call(
        functools.partial(_inproj_body, widths),
        grid=(b, l // tl),
        in_specs=[
            pl.BlockSpec((1, tl, d), lambda i, j: (i, j, 0)),
            vec, vec,
            _const_spec((1, d)),
            _const_spec(w.shape),
        ],
        out_specs=[pl.BlockSpec((1, tl, wd), lambda i, j: (i, j, 0)) for wd in widths],
        out_shape=[jax.ShapeDtypeStruct((b, l, wd), dt) for wd, dt in zip(widths, dtypes)],
        compiler_params=_params(("parallel", "parallel")),
        name="in_proj",
    )(x, shift, scale, g, w)


def _shift_rows(z, prev, nxt, k, row8):
    tl = z.shape[0]
    if k < 0:
        r = pltpu.roll(z, -k, axis=0)
        head = jnp.where(row8 < -k, pltpu.roll(prev, -k, axis=0), r[:V7X_SUBLANES])
        return jnp.concatenate([head, r[V7X_SUBLANES:]], axis=0)
    r = pltpu.roll(z, tl - k, axis=0)
    tail = jnp.where(row8 >= V7X_SUBLANES - k, pltpu.roll(nxt, V7X_SUBLANES - k, axis=0), r[tl - V7X_SUBLANES:])
    return jnp.concatenate([r[:tl - V7X_SUBLANES], tail], axis=0)


def _lru_body(direction, nt, *refs):
    if direction == 0:
        (zx_ref, zp_ref, zn_ref, cw_ref, cb_ref, wd_ref, bd_ref, lam_ref, h0_ref,
         out_ref, hl_ref, a_s, b_s, h_s) = refs
    else:
        (zx_ref, zp_ref, zn_ref, cw_ref, cb_ref, wd_ref, bd_ref, lam_ref, h0_ref, hf_ref, zg_ref,
         out_ref, hl_ref, a_s, b_s, h_s) = refs
    nb, tl, dl = zx_ref.shape
    nslab = dl // V7X_LANES
    j = pl.program_id(1)
    jn = j if direction == 0 else nt - 1 - j

    @pl.when(j == 0)
    def _():
        h_s[...] = h0_ref[...]

    row8 = lax.broadcasted_iota(jnp.int32, (V7X_SUBLANES, dl), 0)
    has_prev = jn > 0
    has_next = jn < nt - 1
    decay = LRU_C * jax.nn.softplus(-lam_ref[...])

    def prep(bi, carry):
        z = zx_ref[bi]
        prev = jnp.where(has_prev, zp_ref[bi], 0.0)
        nxt = jnp.where(has_next, zn_ref[bi], 0.0)
        xa = cb_ref[...] + cw_ref[LRU_CONV_LEFT:LRU_CONV_LEFT + 1, :] * z
        for tap in range(LRU_CONV):
            k = tap - LRU_CONV_LEFT
            if k != 0:
                xa = xa + cw_ref[tap:tap + 1, :] * _shift_rows(z, prev, nxt, k, row8)
        gt = _dot(xa.astype(BF16), wd_ref[...]) + bd_ref[...]
        r = jax.nn.sigmoid(gt[:, :dl])
        gi = jax.nn.sigmoid(gt[:, dl:])
        a = jnp.exp(-decay * r)
        bb = jnp.sqrt(1.0 - a * a) * (gi * xa)
        for s in range(nslab):
            lanes = slice(s * V7X_LANES, (s + 1) * V7X_LANES)
            a_s[s, pl.ds(bi, tl, stride=LRU_BATCH), :] = a[:, lanes]
            b_s[s, pl.ds(bi, tl, stride=LRU_BATCH), :] = bb[:, lanes]
        return carry

    lax.fori_loop(0, nb, prep, 0)

    def step(i, hs):
        t = i if direction == 0 else tl - 1 - i
        rows = pl.ds(pl.multiple_of(t * LRU_BATCH, LRU_BATCH), LRU_BATCH)
        new = []
        for s in range(nslab):
            h = a_s[s, rows, :] * hs[s] + b_s[s, rows, :]
            b_s[s, rows, :] = h
            new.append(h)
        return tuple(new)

    h_init = tuple(h_s[:, s * V7X_LANES:(s + 1) * V7X_LANES] for s in range(nslab))
    h_fin = lax.fori_loop(0, tl, step, h_init, unroll=8)
    h_last = jnp.concatenate(h_fin, axis=-1)
    h_s[...] = h_last
    hl_ref[...] = h_last

    def emit(bi, carry):
        h = jnp.concatenate([b_s[s, pl.ds(bi, tl, stride=LRU_BATCH), :] for s in range(nslab)], axis=-1)
        if direction == 0:
            out_ref[bi] = h
        else:
            gate = jax.nn.gelu(zg_ref[bi], approximate=True)
            out_ref[bi] = ((hf_ref[bi] + h) * gate).astype(out_ref.dtype)
        return carry

    lax.fori_loop(0, nb, emit, 0)


def _lru_call(direction, zx, conv_w, conv_b, wd, bd, lam, h0, hf=None, zg=None):
    b, l, dl = zx.shape
    tl = min(l, LRU_TILE)
    nt = l // tl
    nb = LRU_BATCH
    tiles8 = tl // V7X_SUBLANES

    def nat(j):
        return j if direction == 0 else nt - 1 - j

    main = pl.BlockSpec((nb, tl, dl), lambda g, j: (g, nat(j), 0))
    prev = pl.BlockSpec((nb, V7X_SUBLANES, dl), lambda g, j: (g, jnp.maximum(nat(j) * tiles8 - 1, 0), 0))
    nxt = pl.BlockSpec((nb, V7X_SUBLANES, dl),
                       lambda g, j: (g, jnp.minimum((nat(j) + 1) * tiles8, l // V7X_SUBLANES - 1), 0))
    state = pl.BlockSpec((nb, dl), lambda g, j: (g, 0))
    in_specs = [main, prev, nxt, _const_spec(conv_w.shape), _const_spec((1, dl)), _const_spec(wd.shape),
                _const_spec((1, 2 * dl)), _const_spec((1, dl)), state]
    args = [zx, zx, zx, conv_w, conv_b, wd, bd, lam, h0]
    if direction == 1:
        in_specs += [main, main]
        args += [hf, zg]
    out_dtype = F32 if direction == 0 else BF16
    slab = pltpu.VMEM((dl // V7X_LANES, tl * nb, V7X_LANES), F32)
    return pl.pallas_call(
        functools.partial(_lru_body, direction, nt),
        grid=(b // nb, nt),
        in_specs=in_specs,
        out_specs=[main, state],
        out_shape=[jax.ShapeDtypeStruct((b, l, dl), out_dtype), jax.ShapeDtypeStruct((b, dl), F32)],
        scratch_shapes=[slab, slab, pltpu.VMEM((nb, dl), F32)],
        compiler_params=_params(("parallel", "arbitrary")),
        name=f"rglru_dir{direction}",
    )(*args)


def _fourier_body(scale, zf_ref, wc_ref, cs_ref, o_ref, ab_s):
    l, cf = zf_ref.shape[1], zf_ref.shape[2]
    ab = _dot(zf_ref[0], wc_ref[...])
    ab_s[0:l, :] = ab[:, :cf].astype(BF16)
    ab_s[l:2 * l, :] = ab[:, cf:].astype(BF16)
    o_ref[0] = (_dot(cs_ref[...], ab_s[...]) * scale).astype(o_ref.dtype)


def _dft_tables(l, cf):
    n = np.arange(l)
    ang_l = 2.0 * np.pi * ((n[:, None] * n[None, :]) % l) / l
    cs = np.concatenate([np.cos(ang_l), -np.sin(ang_l)], axis=1).astype(np.float32)
    m = np.arange(cf)
    ang_c = 2.0 * np.pi * ((m[:, None] * m[None, :]) % cf) / cf
    wc = np.concatenate([np.cos(ang_c), np.sin(ang_c)], axis=1).astype(np.float32)
    return jnp.asarray(cs).astype(BF16), jnp.asarray(wc).astype(BF16)


def _fourier_call(zf):
    b, l, cf = zf.shape
    cs, wc = _dft_tables(l, cf)
    blk = pl.BlockSpec((1, l, cf), lambda i: (i, 0, 0))
    return pl.pallas_call(
        functools.partial(_fourier_body, float(1.0 / np.sqrt(l * cf))),
        grid=(b,),
        in_specs=[blk, _const_spec(wc.shape), _const_spec(cs.shape)],
        out_specs=blk,
        out_shape=jax.ShapeDtypeStruct((b, l, cf), BF16),
        scratch_shapes=[pltpu.VMEM((2 * l, cf), BF16)],
        compiler_params=_params(("parallel",)),
        name="fourier_mix",
    )(zf, wc, cs)


def _poolconv_body(zp_ref, zs_ref, pm_ref, ic_ref, pw_ref, ps_ref, sw_ref, yp_ref, ys_ref):
    l, dp = zp_ref.shape[1], zp_ref.shape[2]
    ngroups = pm_ref.shape[0]
    group = lax.broadcasted_iota(jnp.int32, (POOL_TILE, dp), 1) // (dp // ngroups)

    def pool_tile(i, carry):
        rows = pl.ds(pl.multiple_of(i * POOL_TILE, POOL_TILE), POOL_TILE)
        u = zp_ref[0, rows, :]
        hi = u.astype(BF16)
        lo = (u - hi.astype(F32)).astype(BF16)
        ssum = jnp.zeros((POOL_TILE, dp), F32)
        for g in range(ngroups):
            sg = _dot(pm_ref[g], hi) + _dot(pm_ref[g], lo)
            ssum = jnp.where(group == g, sg, ssum)
        p = ssum * ic_ref[...] - u
        yp_ref[0, rows, :] = (_dot(p.astype(BF16), pw_ref[...]) * ps_ref[...]).astype(yp_ref.dtype)
        return carry

    lax.fori_loop(0, l // POOL_TILE, pool_tile, 0)

    ds = zs_ref.shape[2] // 3
    gb = zs_ref[0, :, 0:ds]
    v = zs_ref[0, :, ds:2 * ds] * zs_ref[0, :, 2 * ds:3 * ds]
    row = lax.broadcasted_iota(jnp.int32, (l, ds), 0)
    before = jnp.where(row == 0, 0.0, pltpu.roll(v, 1, axis=0))
    after = jnp.where(row == l - 1, 0.0, pltpu.roll(v, l - 1, axis=0))
    conv = sw_ref[0:1, :] * before + sw_ref[1:2, :] * v + sw_ref[2:3, :] * after
    ys_ref[0] = (gb * conv).astype(ys_ref.dtype)


def _pool_tables(row_len, dp):
    t = np.arange(POOL_TILE)
    r0 = (t // row_len) * row_len
    mats, inv = [], []
    for w in POOL_WINDOWS:
        lo = np.maximum(t - w // 2, r0)
        hi = np.minimum(t + w // 2, r0 + row_len)
        mats.append(((t[None, :] >= lo[:, None]) & (t[None, :] < hi[:, None])).astype(np.float32))
        inv.append(np.repeat((1.0 / (hi - lo))[:, None], dp // len(POOL_WINDOWS), axis=1))
    pm = jnp.asarray(np.stack(mats)).astype(BF16)
    ic = jnp.asarray(np.concatenate(inv, axis=1).astype(np.float32))
    return pm, ic


def _poolconv_call(zp, zs, row_len, pw, pscale, sw):
    b, l, dp = zp.shape
    pm, ic = _pool_tables(min(row_len, POOL_TILE), dp)
    assert row_len <= POOL_TILE and POOL_TILE % row_len == 0 and l % POOL_TILE == 0
    ds = zs.shape[2] // 3
    return pl.pallas_call(
        _poolconv_body,
        grid=(b,),
        in_specs=[
            pl.BlockSpec((1, l, dp), lambda i: (i, 0, 0)),
            pl.BlockSpec((1, l, 3 * ds), lambda i: (i, 0, 0)),
            _const_spec(pm.shape), _const_spec(ic.shape), _const_spec(pw.shape),
            _const_spec((1, dp)), _const_spec(sw.shape),
        ],
        out_specs=[pl.BlockSpec((1, l, dp), lambda i: (i, 0, 0)), pl.BlockSpec((1, l, ds), lambda i: (i, 0, 0))],
        out_shape=[jax.ShapeDtypeStruct((b, l, dp), BF16), jax.ShapeDtypeStruct((b, l, ds), BF16)],
        compiler_params=_params(("parallel",)),
        name="pool_sconv",
    )(zp, zs, pm, ic, pw, pscale, sw)


def _merge_body(x_ref, yl_ref, yf_ref, yp_ref, ys_ref, sh_ref, sc_ref, gt_ref, g_ref,
                wg_ref, wl_ref, wf_ref, wp_ref, ws_ref, wo_ref, o_ref):
    x = x_ref[0]
    d = x.shape[-1]
    u = _norm_mod(x, g_ref[...], sh_ref[0], sc_ref[0]).astype(BF16)
    merged = None
    for i, (y_ref, w_ref) in enumerate(((yl_ref, wl_ref), (yf_ref, wf_ref), (yp_ref, wp_ref), (ys_ref, ws_ref))):
        gate = jax.nn.sigmoid(_dot(u, wg_ref[:, i * d:(i + 1) * d]))
        term = gate * _dot(y_ref[0], w_ref[...])
        merged = term if merged is None else merged + term
    o_ref[0] = x + gt_ref[0] * _dot(merged.astype(BF16), wo_ref[...])


def _merge_call(x, yl, yf, yp, ys, shift, scale, gate, g, wg, wl, wf, wp, ws, wo):
    b, l, d = x.shape
    tm = min(l, 512)
    vec = pl.BlockSpec((1, 1, d), lambda i, j: (i, 0, 0))

    def tok(a):
        return pl.BlockSpec((1, tm, a.shape[2]), lambda i, j: (i, j, 0))

    return pl.pallas_call(
        _merge_body,
        grid=(b, l // tm),
        in_specs=[tok(x), tok(yl), tok(yf), tok(yp), tok(ys), vec, vec, vec, _const_spec((1, d))]
        + [_const_spec(w.shape) for w in (wg, wl, wf, wp, ws, wo)],
        out_specs=tok(x),
        out_shape=jax.ShapeDtypeStruct(x.shape, F32),
        compiler_params=_params(("parallel", "parallel")),
        name="merge_out",
    )(x, yl, yf, yp, ys, shift, scale, gate, g, wg, wl, wf, wp, ws, wo)


def _mlp_body(final_norm, x_ref, sh_ref, sc_ref, gt_ref, g_ref, w1_ref, w2_ref, gf_ref, o_ref):
    x = x_ref[0]
    d = x.shape[-1]
    dff = w1_ref.shape[1]
    u = _norm_mod(x, g_ref[...], sh_ref[0], sc_ref[0]).astype(BF16)
    acc = None
    for k in range(dff // d):
        cols = slice(k * d, (k + 1) * d)
        h = jnp.square(jnp.maximum(_dot(u, w1_ref[:, cols]), 0.0)).astype(BF16)
        part = _dot(h, w2_ref[cols, :])
        acc = part if acc is None else acc + part
    y = x + gt_ref[0] * acc
    if final_norm:
        ms = jnp.mean(y * y, axis=-1, keepdims=True)
        y = (y * lax.rsqrt(ms + EPS)) * gf_ref[...]
    o_ref[0] = y


def _mlp_call(x, shift, scale, gate, g, w1, w2, g_final, final_norm):
    b, l, d = x.shape
    tm = min(l, 512)
    vec = pl.BlockSpec((1, 1, d), lambda i, j: (i, 0, 0))
    tok = pl.BlockSpec((1, tm, d), lambda i, j: (i, j, 0))
    return pl.pallas_call(
        functools.partial(_mlp_body, final_norm),
        grid=(b, l // tm),
        in_specs=[tok, vec, vec, vec, _const_spec((1, d)), _const_spec(w1.shape), _const_spec(w2.shape),
                  _const_spec((1, d))],
        out_specs=tok,
        out_shape=jax.ShapeDtypeStruct(x.shape, F32),
        compiler_params=_params(("parallel", "parallel")),
        name="sqrelu_mlp",
    )(x, shift, scale, gate, g, w1, w2, g_final)


def _block_diag(w):
    h, bs, _ = w.shape
    eye = jnp.eye(h, dtype=w.dtype)
    return (eye[:, None, :, None] * w[:, :, None, :]).reshape(h * bs, h * bs)


def kernel(x, c, ctx, c_ctx, w_mod, b_mod, g_norm1, g_norm2, w_in, lru_conv_w, lru_conv_b, lru_w_a, lru_b_a, lru_w_x, lru_b_x, lru_lam, pool_w, pool_scale, sconv_w, w_br_lru, w_br_fourier, w_br_pool, w_br_sconv, w_out, w_ff1, w_ff2, g_final):
    bn, seq, d = x.shape
    depth = w_mod.shape[0]
    d_lru = lru_conv_w.shape[2]
    d_fourier = w_br_fourier.shape[1]
    d_pool = w_br_pool.shape[1]
    d_sconv = w_br_sconv.shape[1]
    widths = (d_lru, d_lru, d_fourier, d_pool, 3 * d_sconv)
    n_small = sum(widths)
    assert bn % LRU_BATCH == 0 and w_in.shape[2] == n_small + N_BRANCH * d

    pad = (-(bn + 1)) % V7X_SUBLANES
    c_all = jnp.concatenate([c, c_ctx[None, :], jnp.zeros((pad, d), F32)], axis=0)
    mod_all = _mod_call(c_all, w_mod, b_mod).reshape(depth, bn + 1 + pad, N_MOD, 1, d)

    row = lambda v: v.reshape(1, -1)
    g_fin = row(g_final)
    zeros_state = jnp.zeros((bn, d_lru), F32)

    for l in range(depth):
        mx = [mod_all[l, :bn, i] for i in range(N_MOD)]
        mc = [jnp.broadcast_to(mod_all[l, bn:bn + 1, i], (bn, 1, d)) for i in range(N_MOD)]
        w_small = w_in[l][:, :n_small].astype(BF16)
        w_gate = w_in[l][:, n_small:].astype(BF16)
        wd = [jnp.concatenate([_block_diag(lru_w_a[l, dr]), _block_diag(lru_w_x[l, dr])], axis=1).astype(BF16)
              for dr in range(2)]
        bd = [jnp.concatenate([lru_b_a[l, dr], lru_b_x[l, dr]])[None, :] for dr in range(2)]
        lam = [row(lru_lam[l, dr]) for dr in range(2)]
        pw = _block_diag(pool_w[l]).astype(BF16)
        branch_w = [w.astype(BF16) for w in (w_br_lru[l], w_br_fourier[l], w_br_pool[l], w_br_sconv[l])]
        wo = w_out[l].astype(BF16)
        w1 = w_ff1[l].astype(BF16)
        w2 = w_ff2[l].astype(BF16)
        g1, g2 = row(g_norm1[l]), row(g_norm2[l])
        cw, cb = lru_conv_w[l], row(lru_conv_b[l])
        last = l == depth - 1

        def lru(zx, zg, h0):
            hf, hf_last = _lru_call(0, zx, cw, cb, wd[0], bd[0], lam[0], h0[0])
            y, hb_last = _lru_call(1, zx, cw, cb, wd[1], bd[1], lam[1], h0[1], hf, zg)
            return y, (hf_last, hb_last)

        def mixer(s, m, row_len, h0):
            zx, zg, zf, zp, zs = _inproj_call(s, m[0], m[1], g1, w_small, widths, (F32, F32, BF16, F32, F32))
            y_lru, h_last = lru(zx, zg, h0)
            y_f = _fourier_call(zf)
            y_p, y_s = _poolconv_call(zp, zs, row_len, pw, row(pool_scale[l]), sconv_w[l])
            s = _merge_call(s, y_lru, y_f, y_p, y_s, m[0], m[1], m[2], g1, w_gate, *branch_w, wo)
            return s, h_last

        if not last:
            ctx, h_ctx = mixer(ctx, mc, ctx.shape[1], (zeros_state, zeros_state))
            ctx = _mlp_call(ctx, mc[3], mc[4], mc[5], g2, w1, w2, g_fin, False)
        else:
            zx_c, zg_c = _inproj_call(ctx, mc[0], mc[1], g1, w_small[:, :2 * d_lru], widths[:2], (F32, F32))
            _, h_ctx = lru(zx_c, zg_c, (zeros_state, zeros_state))
        x, _ = mixer(x, mx, GRID_W, h_ctx)
        x = _mlp_call(x, mx[3], mx[4], mx[5], g2, w1, w2, g_fin, last)
    return x
```

```python
import functools

import numpy as np
import jax
import jax.numpy as jnp
from jax import lax
from jax.experimental import pallas as pl
from jax.experimental.pallas import tpu as pltpu

F32 = jnp.float32
BF16 = jnp.bfloat16

EPS = 1e-6
LRU_C = 8.0
LRU_CONV = 4
LRU_CONV_LEFT = 2
POOL_WINDOWS = (2, 4, 8, 16)
N_BRANCH = 4
N_MOD = 6
GRID_W = 64
LOG2_E = 1.4426950408889634
SQRT_GUARD = 1e-30

V7X_LANES = 128
V7X_SUBLANES = 8
V7X_VMEM_LIMIT_BYTES = 56 * 1024 * 1024

INPROJ_TILE = 512
MERGE_TILE = 512
MLP_TILE = 512
POOL_TILE = 256
POOL_UNROLL = 4
LRU_TILE = 256
LRU_BATCH = V7X_SUBLANES
LRU_PREP_UNROLL = 4
SCAN_UNROLL = 8
DFT_RADIX = 4


def _params(sem, vmem=V7X_VMEM_LIMIT_BYTES):
    return pltpu.CompilerParams(dimension_semantics=sem, vmem_limit_bytes=vmem)


def _const_spec(shape):
    nd = len(shape)
    return pl.BlockSpec(shape, lambda *_: (0,) * nd, pipeline_mode=pl.Buffered(1))


def _norm_mod(x, g, shift, scale):
    ms = jnp.mean(x * x, axis=-1, keepdims=True)
    y = (x * lax.rsqrt(ms + EPS)) * g
    return y * (1.0 + scale) + shift


def _dot(a, b):
    return jnp.dot(a, b, preferred_element_type=F32)


def _sigmoid(x):
    return 0.5 * jnp.tanh(0.5 * x) + 0.5


def _mod_body(c_ref, w_ref, b_ref, o_ref):
    c = c_ref[...]
    s = c * jax.nn.sigmoid(c)
    o_ref[0] = _dot(s.astype(BF16), w_ref[0].astype(BF16)) + b_ref[0]


def _mod_call(c_all, w_mod, b_mod):
    depth, d, n = w_mod.shape
    rows = c_all.shape[0]
    tn = n // 4
    return pl.pallas_call(
        _mod_body,
        grid=(depth, n // tn),
        in_specs=[
            pl.BlockSpec((rows, d), lambda l, j: (0, 0)),
            pl.BlockSpec((1, d, tn), lambda l, j: (l, 0, j)),
            pl.BlockSpec((1, 1, tn), lambda l, j: (l, 0, j)),
        ],
        out_specs=pl.BlockSpec((1, rows, tn), lambda l, j: (l, 0, j)),
        out_shape=jax.ShapeDtypeStruct((depth, rows, n), F32),
        compiler_params=_params(("parallel", "parallel")),
        name="adaln_mod",
    )(c_all, w_mod, b_mod.reshape(depth, 1, n))


def _shift_rows(z, prev, nxt, k, row8):
    tl = z.shape[0]
    if k < 0:
        r = pltpu.roll(z, -k, axis=0)
        head = jnp.where(row8 < -k, pltpu.roll(prev, -k, axis=0), r[:V7X_SUBLANES])
        return jnp.concatenate([head, r[V7X_SUBLANES:]], axis=0)
    r = pltpu.roll(z, tl - k, axis=0)
    tail = jnp.where(row8 >= V7X_SUBLANES - k, pltpu.roll(nxt, V7X_SUBLANES - k, axis=0), r[tl - V7X_SUBLANES:])
    return jnp.concatenate([r[:tl - V7X_SUBLANES], tail], axis=0)


def _inproj_body(nt, lru_only, x_ref, xp_ref, xn_ref, sh_ref, sc_ref, g_ref, w_ref, cw_ref, cb_ref, xa_ref, *o_refs):
    j = pl.program_id(1)
    tl = x_ref.shape[1]
    dl = xa_ref.shape[2]
    g, sh, sc = g_ref[...], sh_ref[0], sc_ref[0]
    rows = jnp.concatenate([x_ref[0], xp_ref[0], xn_ref[0]], axis=0)
    u_all = _norm_mod(rows, g, sh, sc).astype(BF16)
    z_all = _dot(u_all, w_ref[:, 0:dl])
    z = z_all[:tl]
    prev = jnp.where(j > 0, z_all[tl:tl + V7X_SUBLANES], 0.0)
    nxt = jnp.where(j < nt - 1, z_all[tl + V7X_SUBLANES:], 0.0)
    row8 = lax.broadcasted_iota(jnp.int32, (V7X_SUBLANES, dl), 0)
    xa = cb_ref[...] + cw_ref[LRU_CONV_LEFT:LRU_CONV_LEFT + 1, :] * z
    for tap in range(LRU_CONV):
        k = tap - LRU_CONV_LEFT
        if k != 0:
            xa = xa + cw_ref[tap:tap + 1, :] * _shift_rows(z, prev, nxt, k, row8)
    xa_ref[0] = xa
    if lru_only:
        return
    gz_ref, zf_ref, zp_ref, gb_ref, v_ref = o_refs
    u = u_all[:tl]
    off = dl
    gz_ref[0] = jax.nn.gelu(_dot(u, w_ref[:, off:off + dl]), approximate=True)
    off += dl
    for o_ref in (zf_ref, zp_ref):
        wd = o_ref.shape[2]
        o_ref[0] = _dot(u, w_ref[:, off:off + wd]).astype(o_ref.dtype)
        off += wd
    ds = gb_ref.shape[2]
    zs = _dot(u, w_ref[:, off:off + 3 * ds])
    gb_ref[0] = zs[:, 0:ds]
    v_ref[0] = zs[:, ds:2 * ds] * zs[:, 2 * ds:3 * ds]


def _inproj_call(x, shift, scale, g, w, conv_w, conv_b, widths=None):
    b, l, d = x.shape
    tl = min(l, INPROJ_TILE)
    nt = l // tl
    dl = conv_w.shape[1]
    tiles8 = tl // V7X_SUBLANES
    vec = pl.BlockSpec((1, 1, d), lambda i, j: (i, 0, 0))
    prev = pl.BlockSpec((1, V7X_SUBLANES, d), lambda i, j: (i, jnp.maximum(j * tiles8 - 1, 0), 0))
    nxt = pl.BlockSpec((1, V7X_SUBLANES, d), lambda i, j: (i, jnp.minimum((j + 1) * tiles8, l // V7X_SUBLANES - 1), 0))
    outs = [(dl, F32)]
    if widths is not None:
        _, cf, dp, ds = widths
        outs += [(dl, F32), (cf, BF16), (dp, F32), (ds, F32), (ds, F32)]
    return pl.pallas_call(
        functools.partial(_inproj_body, nt, widths is None),
        grid=(b, nt),
        in_specs=[
            pl.BlockSpec((1, tl, d), lambda i, j: (i, j, 0)), prev, nxt,
            vec, vec,
            _const_spec((1, d)),
            _const_spec(w.shape), _const_spec(conv_w.shape), _const_spec((1, dl)),
        ],
        out_specs=[pl.BlockSpec((1, tl, wd), lambda i, j: (i, j, 0)) for wd, _ in outs],
        out_shape=[jax.ShapeDtypeStruct((b, l, wd), dt) for wd, dt in outs],
        compiler_params=_params(("parallel", "parallel")),
        name="in_proj",
    )(x, x, x, shift, scale, g, w, conv_w, conv_b)


def _lru_body(direction, emit, *refs):
    if not emit:
        (xa_ref, wd_ref, bd_ref, lam_ref, h0_ref, hl_ref, a_s, b_s, h_s) = refs
    elif direction == 0:
        (xa_ref, wd_ref, bd_ref, lam_ref, h0_ref, out_ref, hl_ref, a_s, b_s, h_s) = refs
    else:
        (xa_ref, wd_ref, bd_ref, lam_ref, h0_ref, hf_ref, gz_ref, out_ref, hl_ref, a_s, b_s, h_s) = refs
    nb, tl, dl = xa_ref.shape
    nslab = dl // V7X_LANES

    @pl.when(pl.program_id(1) == 0)
    def _():
        h_s[...] = h0_ref[...]

    rate = (-LRU_C * LOG2_E) * jax.nn.softplus(-lam_ref[...])

    def prep(bi, carry):
        xa = xa_ref[bi]
        gt = _dot(xa.astype(BF16), wd_ref[...]) + bd_ref[...]
        r = _sigmoid(gt[:, :dl])
        gi = _sigmoid(gt[:, dl:])
        a = jnp.exp2(rate * r)
        y = 1.0 - a * a
        bb = (y * lax.rsqrt(jnp.maximum(y, SQRT_GUARD))) * (gi * xa)
        for s in range(nslab):
            lanes = slice(s * V7X_LANES, (s + 1) * V7X_LANES)
            a_s[s, pl.ds(bi, tl, stride=LRU_BATCH), :] = a[:, lanes]
            b_s[s, pl.ds(bi, tl, stride=LRU_BATCH), :] = bb[:, lanes]
        return carry

    lax.fori_loop(0, nb, prep, 0, unroll=LRU_PREP_UNROLL)

    def steps(g, hs):
        first = g * SCAN_UNROLL if direction == 0 else tl - SCAN_UNROLL - g * SCAN_UNROLL
        base = pl.multiple_of(first * LRU_BATCH, SCAN_UNROLL * LRU_BATCH)
        for k in range(SCAN_UNROLL):
            t = k if direction == 0 else SCAN_UNROLL - 1 - k
            rows = pl.ds(base + t * LRU_BATCH, LRU_BATCH)
            new = []
            for s in range(nslab):
                h = a_s[s, rows, :] * hs[s] + b_s[s, rows, :]
                b_s[s, rows, :] = h
                new.append(h)
            hs = tuple(new)
        return hs

    h_init = tuple(h_s[:, s * V7X_LANES:(s + 1) * V7X_LANES] for s in range(nslab))
    h_fin = lax.fori_loop(0, tl // SCAN_UNROLL, steps, h_init)
    h_last = jnp.concatenate(h_fin, axis=-1)
    h_s[...] = h_last
    hl_ref[...] = h_last

    if emit:
        def write(bi, carry):
            h = jnp.concatenate([b_s[s, pl.ds(bi, tl, stride=LRU_BATCH), :] for s in range(nslab)], axis=-1)
            if direction == 0:
                out_ref[bi] = h
            else:
                out_ref[bi] = ((hf_ref[bi] + h) * gz_ref[bi]).astype(out_ref.dtype)
            return carry

        lax.fori_loop(0, nb, write, 0)


def _lru_call(direction, xa, wd, bd, lam, h0, hf=None, gz=None, emit=True):
    b, l, dl = xa.shape
    tl = min(l, LRU_TILE)
    nt = l // tl
    nb = LRU_BATCH
    main = pl.BlockSpec((nb, tl, dl), (lambda g, j: (g, j, 0)) if direction == 0 else (lambda g, j: (g, nt - 1 - j, 0)))
    state = pl.BlockSpec((nb, dl), lambda g, j: (g, 0))
    in_specs = [main, _const_spec(wd.shape), _const_spec((1, 2 * dl)), _const_spec((1, dl)), state]
    args = [xa, wd, bd, lam, h0]
    state_shape = jax.ShapeDtypeStruct((b, dl), F32)
    out_specs, out_shape = [state], [state_shape]
    if emit:
        out_specs, out_shape = [main, state], [jax.ShapeDtypeStruct((b, l, dl), F32 if direction == 0 else BF16), state_shape]
        if direction == 1:
            in_specs += [main, main]
            args += [hf, gz]
    slab = pltpu.VMEM((dl // V7X_LANES, tl * nb, V7X_LANES), F32)
    return pl.pallas_call(
        functools.partial(_lru_body, direction, emit),
        grid=(b // nb, nt),
        in_specs=in_specs,
        out_specs=out_specs,
        out_shape=out_shape,
        scratch_shapes=[slab, slab, pltpu.VMEM((nb, dl), F32)],
        compiler_params=_params(("parallel", "arbitrary")),
        name=f"rglru_dir{direction}",
    )(*args)


def _fourier_body(scale, zf_ref, wc_ref, tab_ref, o_ref, y_s):
    l, cf = zf_ref.shape[1], zf_ref.shape[2]
    lq = l // DFT_RADIX
    ab = _dot(zf_ref[0], wc_ref[...])
    a0, a1, a2, a3 = (ab[q * lq:(q + 1) * lq, :cf] for q in range(DFT_RADIX))
    b0, b1, b2, b3 = (ab[q * lq:(q + 1) * lq, cf:] for q in range(DFT_RADIX))
    g_re = (a0 + a1 + a2 + a3, a0 - b1 - a2 + b3, a0 - a1 + a2 - a3, a0 + b1 - a2 - b3)
    g_im = (-(b0 + b1 + b2 + b3), a3 + b2 - a1 - b0, b1 + b3 - b0 - b2, a1 + b2 - a3 - b0)
    for r in range(DFT_RADIX):
        g = jnp.concatenate([g_re[r], g_im[r]], axis=0).astype(BF16)
        yr = _dot(tab_ref[r], g) * scale
        for s in range(cf // V7X_LANES):
            y_s[s, pl.ds(r, lq, stride=DFT_RADIX), :] = yr[:, s * V7X_LANES:(s + 1) * V7X_LANES]
    o_ref[0] = jnp.concatenate([y_s[s] for s in range(cf // V7X_LANES)], axis=-1).astype(o_ref.dtype)


def _dft_tables(l, cf):
    lq = l // DFT_RADIX
    m = np.arange(lq)
    tabs = []
    for r in range(DFT_RADIX):
        k = DFT_RADIX * np.arange(lq) + r
        ang = 2.0 * np.pi * ((k[:, None] * m[None, :]) % l) / l
        tabs.append(np.concatenate([np.cos(ang), np.sin(ang)], axis=1))
    c = np.arange(cf)
    ang_c = 2.0 * np.pi * ((c[:, None] * c[None, :]) % cf) / cf
    wc = np.concatenate([np.cos(ang_c), np.sin(ang_c)], axis=1)
    return jnp.asarray(np.stack(tabs).astype(np.float32)).astype(BF16), jnp.asarray(wc.astype(np.float32)).astype(BF16)


def _fourier_call(zf):
    b, l, cf = zf.shape
    tabs, wc = _dft_tables(l, cf)
    blk = pl.BlockSpec((1, l, cf), lambda i: (i, 0, 0))
    return pl.pallas_call(
        functools.partial(_fourier_body, float(1.0 / np.sqrt(l * cf))),
        grid=(b,),
        in_specs=[blk, _const_spec(wc.shape), _const_spec(tabs.shape)],
        out_specs=blk,
        out_shape=jax.ShapeDtypeStruct((b, l, cf), BF16),
        scratch_shapes=[pltpu.VMEM((cf // V7X_LANES, l, V7X_LANES), F32)],
        compiler_params=_params(("parallel",)),
        name="fourier_mix",
    )(zf, wc, tabs)


def _poolconv_body(zp_ref, gb_ref, v_ref, pm_ref, ic_ref, pw_ref, ps_ref, sw_ref, yp_ref, ys_ref):
    l, dp = zp_ref.shape[1], zp_ref.shape[2]
    gw = dp // pm_ref.shape[0]
    per_slab = V7X_LANES // gw
    lane_group = lax.broadcasted_iota(jnp.int32, (POOL_TILE, V7X_LANES), 1) // gw

    def pool_tile(i, carry):
        rows = pl.ds(pl.multiple_of(i * POOL_TILE, POOL_TILE), POOL_TILE)
        u = zp_ref[0, rows, :]
        hi = u.astype(BF16)
        lo = (u - hi.astype(F32)).astype(BF16)
        slabs = []
        for s in range(dp // V7X_LANES):
            lanes = slice(s * V7X_LANES, (s + 1) * V7X_LANES)
            both = jnp.concatenate([hi[:, lanes], lo[:, lanes]], axis=1)
            acc = None
            for k in range(per_slab):
                rr = _dot(pm_ref[s * per_slab + k], both)
                sg = rr[:, :V7X_LANES] + rr[:, V7X_LANES:]
                acc = sg if acc is None else jnp.where(lane_group == k, sg, acc)
            slabs.append(acc)
        p = jnp.concatenate(slabs, axis=1) * ic_ref[...] - u
        yp_ref[0, rows, :] = (_dot(p.astype(BF16), pw_ref[...]) * ps_ref[...]).astype(yp_ref.dtype)
        return carry

    lax.fori_loop(0, l // POOL_TILE, pool_tile, 0, unroll=min(POOL_UNROLL, l // POOL_TILE))

    v = v_ref[0]
    row = lax.broadcasted_iota(jnp.int32, v.shape, 0)
    before = jnp.where(row == 0, 0.0, pltpu.roll(v, 1, axis=0))
    after = jnp.where(row == l - 1, 0.0, pltpu.roll(v, l - 1, axis=0))
    conv = sw_ref[0:1, :] * before + sw_ref[1:2, :] * v + sw_ref[2:3, :] * after
    ys_ref[0] = (gb_ref[0] * conv).astype(ys_ref.dtype)


def _pool_tables(row_len, dp):
    t = np.arange(POOL_TILE)
    r0 = (t // row_len) * row_len
    mats, inv = [], []
    for w in POOL_WINDOWS:
        lo = np.maximum(t - w // 2, r0)
        hi = np.minimum(t + w // 2, r0 + row_len)
        mats.append(((t[None, :] >= lo[:, None]) & (t[None, :] < hi[:, None])).astype(np.float32))
        inv.append(np.repeat((1.0 / (hi - lo))[:, None], dp // len(POOL_WINDOWS), axis=1))
    pm = jnp.asarray(np.stack(mats)).astype(BF16)
    ic = jnp.asarray(np.concatenate(inv, axis=1).astype(np.float32))
    return pm, ic


def _poolconv_call(zp, gb, v, row_len, pw, pscale, sw):
    b, l, dp = zp.shape
    assert row_len <= POOL_TILE and POOL_TILE % row_len == 0 and l % POOL_TILE == 0
    assert V7X_LANES % (dp // len(POOL_WINDOWS)) == 0
    pm, ic = _pool_tables(row_len, dp)
    ds = gb.shape[2]

    def seq(width):
        return pl.BlockSpec((1, l, width), lambda i: (i, 0, 0))

    return pl.pallas_call(
        _poolconv_body,
        grid=(b,),
        in_specs=[seq(dp), seq(ds), seq(ds), _const_spec(pm.shape), _const_spec(ic.shape), _const_spec(pw.shape),
                  _const_spec((1, dp)), _const_spec(sw.shape)],
        out_specs=[seq(dp), seq(ds)],
        out_shape=[jax.ShapeDtypeStruct((b, l, dp), BF16), jax.ShapeDtypeStruct((b, l, ds), BF16)],
        compiler_params=_params(("parallel",)),
        name="pool_sconv",
    )(zp, gb, v, pm, ic, pw, pscale, sw)


def _merge_body(x_ref, yl_ref, yf_ref, yp_ref, ys_ref, sh_ref, sc_ref, gt_ref, g_ref,
                wg_ref, wl_ref, wf_ref, wp_ref, ws_ref, wo_ref, o_ref):
    x = x_ref[0]
    d = x.shape[-1]
    u = _norm_mod(x, g_ref[...], sh_ref[0], sc_ref[0]).astype(BF16)
    merged = None
    for i, (y_ref, w_ref) in enumerate(((yl_ref, wl_ref), (yf_ref, wf_ref), (yp_ref, wp_ref), (ys_ref, ws_ref))):
        gate = jax.nn.sigmoid(_dot(u, wg_ref[:, i * d:(i + 1) * d]))
        term = gate * _dot(y_ref[0], w_ref[...])
        merged = term if merged is None else merged + term
    o_ref[0] = x + gt_ref[0] * _dot(merged.astype(BF16), wo_ref[...])


def _merge_call(x, yl, yf, yp, ys, shift, scale, gate, g, wg, wl, wf, wp, ws, wo):
    b, l, d = x.shape
    tm = min(l, MERGE_TILE)
    vec = pl.BlockSpec((1, 1, d), lambda i, j: (i, 0, 0))

    def tok(a):
        return pl.BlockSpec((1, tm, a.shape[2]), lambda i, j: (i, j, 0))

    return pl.pallas_call(
        _merge_body,
        grid=(b, l // tm),
        in_specs=[tok(x), tok(yl), tok(yf), tok(yp), tok(ys), vec, vec, vec, _const_spec((1, d))]
        + [_const_spec(w.shape) for w in (wg, wl, wf, wp, ws, wo)],
        out_specs=tok(x),
        out_shape=jax.ShapeDtypeStruct(x.shape, F32),
        compiler_params=_params(("parallel", "parallel")),
        name="merge_out",
    )(x, yl, yf, yp, ys, shift, scale, gate, g, wg, wl, wf, wp, ws, wo)


def _mlp_body(final_norm, x_ref, sh_ref, sc_ref, gt_ref, g_ref, w1_ref, w2_ref, gf_ref, o_ref):
    x = x_ref[0]
    d = x.shape[-1]
    dff = w1_ref.shape[1]
    u = _norm_mod(x, g_ref[...], sh_ref[0], sc_ref[0]).astype(BF16)
    acc = None
    for k in range(dff // d):
        cols = slice(k * d, (k + 1) * d)
        h = jnp.square(jnp.maximum(_dot(u, w1_ref[:, cols]), 0.0)).astype(BF16)
        part = _dot(h, w2_ref[cols, :])
        acc = part if acc is None else acc + part
    y = x + gt_ref[0] * acc
    if final_norm:
        ms = jnp.mean(y * y, axis=-1, keepdims=True)
        y = (y * lax.rsqrt(ms + EPS)) * gf_ref[...]
    o_ref[0] = y


def _mlp_call(x, shift, scale, gate, g, w1, w2, g_final, final_norm):
    b, l, d = x.shape
    tm = min(l, MLP_TILE)
    vec = pl.BlockSpec((1, 1, d), lambda i, j: (i, 0, 0))
    tok = pl.BlockSpec((1, tm, d), lambda i, j: (i, j, 0))
    return pl.pallas_call(
        functools.partial(_mlp_body, final_norm),
        grid=(b, l // tm),
        in_specs=[tok, vec, vec, vec, _const_spec((1, d)), _const_spec(w1.shape), _const_spec(w2.shape),
                  _const_spec((1, d))],
        out_specs=tok,
        out_shape=jax.ShapeDtypeStruct(x.shape, F32),
        compiler_params=_params(("parallel", "parallel")),
        name="sqrelu_mlp",
    )(x, shift, scale, gate, g, w1, w2, g_final)


def _block_diag(w):
    h, bs, _ = w.shape
    eye = jnp.eye(h, dtype=w.dtype)
    return (eye[:, None, :, None] * w[:, :, None, :]).reshape(h * bs, h * bs)


def kernel(x, c, ctx, c_ctx, w_mod, b_mod, g_norm1, g_norm2, w_in, lru_conv_w, lru_conv_b, lru_w_a, lru_b_a, lru_w_x, lru_b_x, lru_lam, pool_w, pool_scale, sconv_w, w_br_lru, w_br_fourier, w_br_pool, w_br_sconv, w_out, w_ff1, w_ff2, g_final):
    bn, seq, d = x.shape
    depth = w_mod.shape[0]
    d_lru = lru_conv_w.shape[2]
    widths = (d_lru, w_br_fourier.shape[1], w_br_pool.shape[1], w_br_sconv.shape[1])
    n_small = 2 * widths[0] + widths[1] + widths[2] + 3 * widths[3]
    assert bn % LRU_BATCH == 0 and w_in.shape[2] == n_small + N_BRANCH * d

    pad = (-(bn + 1)) % V7X_SUBLANES
    c_all = jnp.concatenate([c, c_ctx[None, :], jnp.zeros((pad, d), F32)], axis=0)
    mod_all = _mod_call(c_all, w_mod, b_mod).reshape(depth, bn + 1 + pad, N_MOD, 1, d)

    row = lambda v: v.reshape(1, -1)
    g_fin = row(g_final)
    zeros_state = jnp.zeros((bn, d_lru), F32)

    for l in range(depth):
        mx = [mod_all[l, :bn, i] for i in range(N_MOD)]
        mc = [jnp.broadcast_to(mod_all[l, bn:bn + 1, i], (bn, 1, d)) for i in range(N_MOD)]
        w_small = w_in[l][:, :n_small].astype(BF16)
        w_gate = w_in[l][:, n_small:].astype(BF16)
        wd = [jnp.concatenate([_block_diag(lru_w_a[l, dr]), _block_diag(lru_w_x[l, dr])], axis=1).astype(BF16)
              for dr in range(2)]
        bd = [jnp.concatenate([lru_b_a[l, dr], lru_b_x[l, dr]])[None, :] for dr in range(2)]
        lam = [row(lru_lam[l, dr]) for dr in range(2)]
        pw = _block_diag(pool_w[l]).astype(BF16)
        branch_w = [w.astype(BF16) for w in (w_br_lru[l], w_br_fourier[l], w_br_pool[l], w_br_sconv[l])]
        wo = w_out[l].astype(BF16)
        w1 = w_ff1[l].astype(BF16)
        w2 = w_ff2[l].astype(BF16)
        g1, g2 = row(g_norm1[l]), row(g_norm2[l])
        cw, cb = lru_conv_w[l], row(lru_conv_b[l])
        last = l == depth - 1

        def mixer(s, m, row_len, h0):
            xa, gz, zf, zp, gb, v = _inproj_call(s, m[0], m[1], g1, w_small, cw, cb, widths)
            hf, hf_last = _lru_call(0, xa, wd[0], bd[0], lam[0], h0[0])
            y_lru, hb_last = _lru_call(1, xa, wd[1], bd[1], lam[1], h0[1], hf=hf, gz=gz)
            y_f = _fourier_call(zf)
            y_p, y_s = _poolconv_call(zp, gb, v, row_len, pw, row(pool_scale[l]), sconv_w[l])
            s = _merge_call(s, y_lru, y_f, y_p, y_s, m[0], m[1], m[2], g1, w_gate, *branch_w, wo)
            return s, (hf_last, hb_last)

        if not last:
            ctx, h_ctx = mixer(ctx, mc, ctx.shape[1], (zeros_state, zeros_state))
            ctx = _mlp_call(ctx, mc[3], mc[4], mc[5], g2, w1, w2, g_fin, False)
        else:
            (xa_c,) = _inproj_call(ctx, mc[0], mc[1], g1, w_small[:, :d_lru], cw, cb)
            h_ctx = tuple(_lru_call(dr, xa_c, wd[dr], bd[dr], lam[dr], zeros_state, emit=False)[0] for dr in range(2))
        x, _ = mixer(x, mx, GRID_W, h_ctx)
        x = _mlp_call(x, mx[3], mx[4], mx[5], g2, w1, w2, g_fin, last)
    return x
```

```python
import functools

import numpy as np
import jax
import jax.numpy as jnp
from jax import lax
from jax.experimental import pallas as pl
from jax.experimental.pallas import tpu as pltpu

F32 = jnp.float32
BF16 = jnp.bfloat16

EPS = 1e-6
LRU_C = 8.0
LRU_CONV = 4
LRU_CONV_LEFT = 2
POOL_WINDOWS = (2, 4, 8, 16)
N_BRANCH = 4
N_MOD = 6
GRID_W = 64
LOG2_E = 1.4426950408889634
SQRT_GUARD = 1e-30

V7X_LANES = 128
V7X_SUBLANES = 8
V7X_MXU_DIM = 256
V7X_VMEM_LIMIT_BYTES = 56 * 1024 * 1024

INPROJ_TILE = 1024
MERGE_TILE = 1024
MLP_TILE = 1024
POOL_TILE = 256
POOL_UNROLL = 4
LRU_TILE = 256
LRU_BATCH = V7X_SUBLANES
LRU_PREP_UNROLL = 4
SCAN_UNROLL = 8
DFT_RADIX = 4


def _params(sem, vmem=V7X_VMEM_LIMIT_BYTES):
    return pltpu.CompilerParams(dimension_semantics=sem, vmem_limit_bytes=vmem)


def _const_spec(shape):
    nd = len(shape)
    return pl.BlockSpec(shape, lambda *_: (0,) * nd, pipeline_mode=pl.Buffered(1))


def _norm_mod(x, g, shift, scale):
    ms = jnp.mean(x * x, axis=-1, keepdims=True)
    y = (x * lax.rsqrt(ms + EPS)) * g
    return y * (1.0 + scale) + shift


def _dot(a, b):
    return jnp.dot(a, b, preferred_element_type=F32)


def _mod_body(c_ref, w_ref, b_ref, o_ref):
    c = c_ref[...]
    s = c * jax.nn.sigmoid(c)
    o_ref[0] = _dot(s.astype(BF16), w_ref[0].astype(BF16)) + b_ref[0]


def _mod_call(c_all, w_mod, b_mod):
    depth, d, n = w_mod.shape
    rows = c_all.shape[0]
    tn = n // 4
    return pl.pallas_call(
        _mod_body,
        grid=(depth, n // tn),
        in_specs=[
            pl.BlockSpec((rows, d), lambda l, j: (0, 0)),
            pl.BlockSpec((1, d, tn), lambda l, j: (l, 0, j)),
            pl.BlockSpec((1, 1, tn), lambda l, j: (l, 0, j)),
        ],
        out_specs=pl.BlockSpec((1, rows, tn), lambda l, j: (l, 0, j)),
        out_shape=jax.ShapeDtypeStruct((depth, rows, n), F32),
        compiler_params=_params(("parallel", "parallel")),
        name="adaln_mod",
    )(c_all, w_mod, b_mod.reshape(depth, 1, n))


def _shift_rows(z, prev, nxt, k, row8):
    tl = z.shape[0]
    if k < 0:
        r = pltpu.roll(z, -k, axis=0)
        head = jnp.where(row8 < -k, pltpu.roll(prev, -k, axis=0), r[:V7X_SUBLANES])
        return jnp.concatenate([head, r[V7X_SUBLANES:]], axis=0)
    r = pltpu.roll(z, tl - k, axis=0)
    tail = jnp.where(row8 >= V7X_SUBLANES - k, pltpu.roll(nxt, V7X_SUBLANES - k, axis=0), r[tl - V7X_SUBLANES:])
    return jnp.concatenate([r[:tl - V7X_SUBLANES], tail], axis=0)


def _inproj_body(nt, lru_only, x_ref, xp_ref, xn_ref, sh_ref, sc_ref, g_ref, w_ref, cw_ref, cb_ref, xa_ref, *o_refs):
    j = pl.program_id(1)
    tl = x_ref.shape[1]
    dl = xa_ref.shape[2]
    g, sh, sc = g_ref[...], sh_ref[0], sc_ref[0]
    rows = jnp.concatenate([x_ref[0], xp_ref[0], xn_ref[0]], axis=0)
    u_all = _norm_mod(rows, g, sh, sc).astype(BF16)
    z_all = _dot(u_all, w_ref[:, 0:dl])
    z = z_all[:tl]
    prev = jnp.where(j > 0, z_all[tl:tl + V7X_SUBLANES], 0.0)
    nxt = jnp.where(j < nt - 1, z_all[tl + V7X_SUBLANES:], 0.0)
    row8 = lax.broadcasted_iota(jnp.int32, (V7X_SUBLANES, dl), 0)
    xa = cb_ref[...] + cw_ref[LRU_CONV_LEFT:LRU_CONV_LEFT + 1, :] * z
    for tap in range(LRU_CONV):
        k = tap - LRU_CONV_LEFT
        if k != 0:
            xa = xa + cw_ref[tap:tap + 1, :] * _shift_rows(z, prev, nxt, k, row8)
    xa_ref[0] = xa
    if lru_only:
        return
    gz_ref, zf_ref, zp_ref, gb_ref, v_ref = o_refs
    u = u_all[:tl]
    off = dl
    gz_ref[0] = jax.nn.gelu(_dot(u, w_ref[:, off:off + dl]), approximate=True)
    off += dl
    for o_ref in (zf_ref, zp_ref):
        wd = o_ref.shape[2]
        o_ref[0] = _dot(u, w_ref[:, off:off + wd]).astype(o_ref.dtype)
        off += wd
    ds = gb_ref.shape[2]
    zs = _dot(u, w_ref[:, off:off + 3 * ds])
    gb_ref[0] = zs[:, 0:ds]
    v_ref[0] = zs[:, ds:2 * ds] * zs[:, 2 * ds:3 * ds]


def _inproj_call(x, shift, scale, g, w, conv_w, conv_b, widths=None):
    b, l, d = x.shape
    tl = min(l, INPROJ_TILE)
    nt = l // tl
    dl = conv_w.shape[1]
    tiles8 = tl // V7X_SUBLANES
    vec = pl.BlockSpec((1, 1, d), lambda i, j: (i, 0, 0))
    prev = pl.BlockSpec((1, V7X_SUBLANES, d), lambda i, j: (i, jnp.maximum(j * tiles8 - 1, 0), 0))
    nxt = pl.BlockSpec((1, V7X_SUBLANES, d), lambda i, j: (i, jnp.minimum((j + 1) * tiles8, l // V7X_SUBLANES - 1), 0))
    outs = [(dl, F32)]
    if widths is not None:
        _, cf, dp, ds = widths
        outs += [(dl, F32), (cf, BF16), (dp, F32), (ds, F32), (ds, F32)]
    return pl.pallas_call(
        functools.partial(_inproj_body, nt, widths is None),
        grid=(b, nt),
        in_specs=[
            pl.BlockSpec((1, tl, d), lambda i, j: (i, j, 0)), prev, nxt,
            vec, vec,
            _const_spec((1, d)),
            _const_spec(w.shape), _const_spec(conv_w.shape), _const_spec((1, dl)),
        ],
        out_specs=[pl.BlockSpec((1, tl, wd), lambda i, j: (i, j, 0)) for wd, _ in outs],
        out_shape=[jax.ShapeDtypeStruct((b, l, wd), dt) for wd, dt in outs],
        compiler_params=_params(("parallel", "parallel")),
        name="in_proj",
    )(x, x, x, shift, scale, g, w, conv_w, conv_b)


def _lru_body(direction, emit, *refs):
    if not emit:
        (xa_ref, wd_ref, bd_ref, lam_ref, h0_ref, hl_ref, a_s, b_s, h_s) = refs
    elif direction == 0:
        (xa_ref, wd_ref, bd_ref, lam_ref, h0_ref, out_ref, hl_ref, a_s, b_s, h_s) = refs
    else:
        (xa_ref, wd_ref, bd_ref, lam_ref, h0_ref, hf_ref, gz_ref, out_ref, hl_ref, a_s, b_s, h_s) = refs
    nb, tl, dl = xa_ref.shape
    nslab = dl // V7X_LANES

    @pl.when(pl.program_id(1) == 0)
    def _():
        h_s[...] = h0_ref[...]

    half_rate = (-0.5 * LRU_C * LOG2_E) * jax.nn.softplus(-lam_ref[...])
    ngroups = wd_ref.shape[0]
    gw = dl // ngroups

    def prep(bi, carry):
        xa = xa_ref[bi]
        xb = xa.astype(BF16)
        parts = [_dot(xb[:, g * gw:(g + 1) * gw], wd_ref[g]) for g in range(ngroups)]
        ta = jnp.tanh(jnp.concatenate([p[:, :gw] for p in parts], axis=1) + bd_ref[:, :dl])
        tx = jnp.tanh(jnp.concatenate([p[:, gw:] for p in parts], axis=1) + bd_ref[:, dl:])
        a = jnp.exp2(half_rate * ta + half_rate)
        y = 1.0 - a * a
        hx = 0.5 * xa
        bb = (y * lax.rsqrt(jnp.maximum(y, SQRT_GUARD))) * (hx * tx + hx)
        for s in range(nslab):
            lanes = slice(s * V7X_LANES, (s + 1) * V7X_LANES)
            a_s[s, pl.ds(bi, tl, stride=LRU_BATCH), :] = a[:, lanes]
            b_s[s, pl.ds(bi, tl, stride=LRU_BATCH), :] = bb[:, lanes]
        return carry

    lax.fori_loop(0, nb, prep, 0, unroll=LRU_PREP_UNROLL)

    def steps(g, hs):
        first = g * SCAN_UNROLL if direction == 0 else tl - SCAN_UNROLL - g * SCAN_UNROLL
        base = pl.multiple_of(first * LRU_BATCH, SCAN_UNROLL * LRU_BATCH)
        for k in range(SCAN_UNROLL):
            t = k if direction == 0 else SCAN_UNROLL - 1 - k
            rows = pl.ds(base + t * LRU_BATCH, LRU_BATCH)
            new = []
            for s in range(nslab):
                h = a_s[s, rows, :] * hs[s] + b_s[s, rows, :]
                b_s[s, rows, :] = h
                new.append(h)
            hs = tuple(new)
        return hs

    h_init = tuple(h_s[:, s * V7X_LANES:(s + 1) * V7X_LANES] for s in range(nslab))
    h_fin = lax.fori_loop(0, tl // SCAN_UNROLL, steps, h_init)
    h_last = jnp.concatenate(h_fin, axis=-1)
    h_s[...] = h_last
    hl_ref[...] = h_last

    if emit:
        def write(bi, carry):
            h = jnp.concatenate([b_s[s, pl.ds(bi, tl, stride=LRU_BATCH), :] for s in range(nslab)], axis=-1)
            if direction == 0:
                out_ref[bi] = h
            else:
                out_ref[bi] = ((hf_ref[bi] + h) * gz_ref[bi]).astype(out_ref.dtype)
            return carry

        lax.fori_loop(0, nb, write, 0)


def _lru_call(direction, xa, wd, bd, lam, h0, hf=None, gz=None, emit=True):
    b, l, dl = xa.shape
    tl = min(l, LRU_TILE)
    nt = l // tl
    nb = LRU_BATCH
    main = pl.BlockSpec((nb, tl, dl), (lambda g, j: (g, j, 0)) if direction == 0 else (lambda g, j: (g, nt - 1 - j, 0)))
    state = pl.BlockSpec((nb, dl), lambda g, j: (g, 0))
    in_specs = [main, _const_spec(wd.shape), _const_spec((1, 2 * dl)), _const_spec((1, dl)), state]
    args = [xa, wd, bd, lam, h0]
    state_shape = jax.ShapeDtypeStruct((b, dl), F32)
    out_specs, out_shape = [state], [state_shape]
    if emit:
        out_specs, out_shape = [main, state], [jax.ShapeDtypeStruct((b, l, dl), F32 if direction == 0 else BF16), state_shape]
        if direction == 1:
            in_specs += [main, main]
            args += [hf, gz]
    slab = pltpu.VMEM((dl // V7X_LANES, tl * nb, V7X_LANES), F32)
    return pl.pallas_call(
        functools.partial(_lru_body, direction, emit),
        grid=(b // nb, nt),
        in_specs=in_specs,
        out_specs=out_specs,
        out_shape=out_shape,
        scratch_shapes=[slab, slab, pltpu.VMEM((nb, dl), F32)],
        compiler_params=_params(("parallel", "arbitrary")),
        name=f"rglru_dir{direction}",
    )(*args)


def _fourier_body(scale, zf_ref, wc_ref, tab_ref, o_ref, y_s):
    l, cf = zf_ref.shape[1], zf_ref.shape[2]
    lq = l // DFT_RADIX
    ab = _dot(zf_ref[0], wc_ref[...])
    a0, a1, a2, a3 = (ab[q * lq:(q + 1) * lq, :cf] for q in range(DFT_RADIX))
    b0, b1, b2, b3 = (ab[q * lq:(q + 1) * lq, cf:] for q in range(DFT_RADIX))
    g_re = (a0 + a1 + a2 + a3, a0 - b1 - a2 + b3, a0 - a1 + a2 - a3, a0 + b1 - a2 - b3)
    g_im = (-(b0 + b1 + b2 + b3), a3 + b2 - a1 - b0, b1 + b3 - b0 - b2, a1 + b2 - a3 - b0)
    for r in range(DFT_RADIX):
        g = jnp.concatenate([g_re[r], g_im[r]], axis=0).astype(BF16)
        yr = _dot(tab_ref[r], g) * scale
        for s in range(cf // V7X_LANES):
            y_s[s, pl.ds(r, lq, stride=DFT_RADIX), :] = yr[:, s * V7X_LANES:(s + 1) * V7X_LANES]
    o_ref[0] = jnp.concatenate([y_s[s] for s in range(cf // V7X_LANES)], axis=-1).astype(o_ref.dtype)


def _dft_tables(l, cf):
    lq = l // DFT_RADIX
    m = np.arange(lq)
    tabs = []
    for r in range(DFT_RADIX):
        k = DFT_RADIX * np.arange(lq) + r
        ang = 2.0 * np.pi * ((k[:, None] * m[None, :]) % l) / l
        tabs.append(np.concatenate([np.cos(ang), np.sin(ang)], axis=1))
    c = np.arange(cf)
    ang_c = 2.0 * np.pi * ((c[:, None] * c[None, :]) % cf) / cf
    wc = np.concatenate([np.cos(ang_c), np.sin(ang_c)], axis=1)
    return jnp.asarray(np.stack(tabs).astype(np.float32)).astype(BF16), jnp.asarray(wc.astype(np.float32)).astype(BF16)


def _fourier_call(zf):
    b, l, cf = zf.shape
    tabs, wc = _dft_tables(l, cf)
    blk = pl.BlockSpec((1, l, cf), lambda i: (i, 0, 0))
    return pl.pallas_call(
        functools.partial(_fourier_body, float(1.0 / np.sqrt(l * cf))),
        grid=(b,),
        in_specs=[blk, _const_spec(wc.shape), _const_spec(tabs.shape)],
        out_specs=blk,
        out_shape=jax.ShapeDtypeStruct((b, l, cf), BF16),
        scratch_shapes=[pltpu.VMEM((cf // V7X_LANES, l, V7X_LANES), F32)],
        compiler_params=_params(("parallel",)),
        name="fourier_mix",
    )(zf, wc, tabs)


def _poolconv_body(zp_ref, gb_ref, v_ref, pm_ref, ic_ref, pw_ref, ps_ref, sw_ref, yp_ref, ys_ref):
    l, dp = zp_ref.shape[1], zp_ref.shape[2]
    gw = dp // pm_ref.shape[0]
    per_slab = V7X_LANES // gw
    lane_group = lax.broadcasted_iota(jnp.int32, (POOL_TILE, V7X_LANES), 1) // gw

    def pool_tile(i, carry):
        rows = pl.ds(pl.multiple_of(i * POOL_TILE, POOL_TILE), POOL_TILE)
        u = zp_ref[0, rows, :]
        hi = u.astype(BF16)
        lo = (u - hi.astype(F32)).astype(BF16)
        slabs = []
        for s in range(dp // V7X_LANES):
            lanes = slice(s * V7X_LANES, (s + 1) * V7X_LANES)
            both = jnp.concatenate([hi[:, lanes], lo[:, lanes]], axis=1)
            acc = None
            for k in range(per_slab):
                rr = _dot(pm_ref[s * per_slab + k], both)
                sg = rr[:, :V7X_LANES] + rr[:, V7X_LANES:]
                acc = sg if acc is None else jnp.where(lane_group == k, sg, acc)
            slabs.append(acc)
        p = jnp.concatenate(slabs, axis=1) * ic_ref[...] - u
        yp_ref[0, rows, :] = (_dot(p.astype(BF16), pw_ref[...]) * ps_ref[...]).astype(yp_ref.dtype)
        return carry

    lax.fori_loop(0, l // POOL_TILE, pool_tile, 0, unroll=min(POOL_UNROLL, l // POOL_TILE))

    v = v_ref[0]
    row8 = lax.broadcasted_iota(jnp.int32, (V7X_SUBLANES, v.shape[1]), 0)
    zeros8 = jnp.zeros((V7X_SUBLANES, v.shape[1]), F32)
    conv = (sw_ref[0:1, :] * _shift_rows(v, zeros8, zeros8, -1, row8) + sw_ref[1:2, :] * v
            + sw_ref[2:3, :] * _shift_rows(v, zeros8, zeros8, 1, row8))
    ys_ref[0] = (gb_ref[0] * conv).astype(ys_ref.dtype)


def _pool_tables(row_len, dp):
    t = np.arange(POOL_TILE)
    r0 = (t // row_len) * row_len
    mats, inv = [], []
    for w in POOL_WINDOWS:
        lo = np.maximum(t - w // 2, r0)
        hi = np.minimum(t + w // 2, r0 + row_len)
        mats.append(((t[None, :] >= lo[:, None]) & (t[None, :] < hi[:, None])).astype(np.float32))
        inv.append(np.repeat((1.0 / (hi - lo))[:, None], dp // len(POOL_WINDOWS), axis=1))
    pm = jnp.asarray(np.stack(mats)).astype(BF16)
    ic = jnp.asarray(np.concatenate(inv, axis=1).astype(np.float32))
    return pm, ic


def _poolconv_call(zp, gb, v, row_len, pw, pscale, sw):
    b, l, dp = zp.shape
    assert row_len <= POOL_TILE and POOL_TILE % row_len == 0 and l % POOL_TILE == 0
    assert V7X_LANES % (dp // len(POOL_WINDOWS)) == 0
    pm, ic = _pool_tables(row_len, dp)
    ds = gb.shape[2]

    def seq(width):
        return pl.BlockSpec((1, l, width), lambda i: (i, 0, 0))

    return pl.pallas_call(
        _poolconv_body,
        grid=(b,),
        in_specs=[seq(dp), seq(ds), seq(ds), _const_spec(pm.shape), _const_spec(ic.shape), _const_spec(pw.shape),
                  _const_spec((1, dp)), _const_spec(sw.shape)],
        out_specs=[seq(dp), seq(ds)],
        out_shape=[jax.ShapeDtypeStruct((b, l, dp), BF16), jax.ShapeDtypeStruct((b, l, ds), BF16)],
        compiler_params=_params(("parallel",)),
        name="pool_sconv",
    )(zp, gb, v, pm, ic, pw, pscale, sw)


def _merge_body(x_ref, yl_ref, yf_ref, yp_ref, ys_ref, sh_ref, sc_ref, gt_ref, g_ref,
                wg_ref, wl_ref, wf_ref, wp_ref, ws_ref, wo_ref, o_ref):
    x = x_ref[0]
    d = x.shape[-1]
    u = _norm_mod(x, g_ref[...], sh_ref[0], sc_ref[0]).astype(BF16)
    merged = None
    for i, (y_ref, w_ref) in enumerate(((yl_ref, wl_ref), (yf_ref, wf_ref), (yp_ref, wp_ref), (ys_ref, ws_ref))):
        gate = jax.nn.sigmoid(_dot(u, wg_ref[:, i * d:(i + 1) * d]))
        term = gate * _dot(y_ref[0], w_ref[...])
        merged = term if merged is None else merged + term
    o_ref[0] = x + gt_ref[0] * _dot(merged.astype(BF16), wo_ref[...])


def _merge_call(x, yl, yf, yp, ys, shift, scale, gate, g, wg, wl, wf, wp, ws, wo):
    b, l, d = x.shape
    tm = min(l, MERGE_TILE)
    vec = pl.BlockSpec((1, 1, d), lambda i, j: (i, 0, 0))

    def tok(a):
        return pl.BlockSpec((1, tm, a.shape[2]), lambda i, j: (i, j, 0))

    return pl.pallas_call(
        _merge_body,
        grid=(b, l // tm),
        in_specs=[tok(x), tok(yl), tok(yf), tok(yp), tok(ys), vec, vec, vec, _const_spec((1, d))]
        + [_const_spec(w.shape) for w in (wg, wl, wf, wp, ws, wo)],
        out_specs=tok(x),
        out_shape=jax.ShapeDtypeStruct(x.shape, F32),
        compiler_params=_params(("parallel", "parallel")),
        name="merge_out",
    )(x, yl, yf, yp, ys, shift, scale, gate, g, wg, wl, wf, wp, ws, wo)


def _mlp_body(final_norm, x_ref, sh_ref, sc_ref, gt_ref, g_ref, w1_ref, w2_ref, gf_ref, o_ref):
    x = x_ref[0]
    d = x.shape[-1]
    dff = w1_ref.shape[1]
    u = _norm_mod(x, g_ref[...], sh_ref[0], sc_ref[0]).astype(BF16)
    acc = None
    for k in range(dff // d):
        cols = slice(k * d, (k + 1) * d)
        h = jnp.square(jnp.maximum(_dot(u, w1_ref[:, cols]), 0.0)).astype(BF16)
        part = _dot(h, w2_ref[cols, :])
        acc = part if acc is None else acc + part
    y = x + gt_ref[0] * acc
    if final_norm:
        ms = jnp.mean(y * y, axis=-1, keepdims=True)
        y = (y * lax.rsqrt(ms + EPS)) * gf_ref[...]
    o_ref[0] = y


def _mlp_call(x, shift, scale, gate, g, w1, w2, g_final, final_norm):
    b, l, d = x.shape
    tm = min(l, MLP_TILE)
    vec = pl.BlockSpec((1, 1, d), lambda i, j: (i, 0, 0))
    tok = pl.BlockSpec((1, tm, d), lambda i, j: (i, j, 0))
    return pl.pallas_call(
        functools.partial(_mlp_body, final_norm),
        grid=(b, l // tm),
        in_specs=[tok, vec, vec, vec, _const_spec((1, d)), _const_spec(w1.shape), _const_spec(w2.shape),
                  _const_spec((1, d))],
        out_specs=tok,
        out_shape=jax.ShapeDtypeStruct(x.shape, F32),
        compiler_params=_params(("parallel", "parallel")),
        name="sqrelu_mlp",
    )(x, shift, scale, gate, g, w1, w2, g_final)


def _block_diag(w):
    h, bs, _ = w.shape
    eye = jnp.eye(h, dtype=w.dtype)
    return (eye[:, None, :, None] * w[:, :, None, :]).reshape(h * bs, h * bs)


def _lru_gate_weights(w_a, w_x):
    h, bs, _ = w_a.shape
    per = max(V7X_MXU_DIM // bs, 1)
    groups = [jnp.concatenate([_block_diag(w_a[g:g + per]), _block_diag(w_x[g:g + per])], axis=1)
              for g in range(0, h, per)]
    return (0.5 * jnp.stack(groups)).astype(BF16)


def kernel(x, c, ctx, c_ctx, w_mod, b_mod, g_norm1, g_norm2, w_in, lru_conv_w, lru_conv_b, lru_w_a, lru_b_a, lru_w_x, lru_b_x, lru_lam, pool_w, pool_scale, sconv_w, w_br_lru, w_br_fourier, w_br_pool, w_br_sconv, w_out, w_ff1, w_ff2, g_final):
    bn, seq, d = x.shape
    depth = w_mod.shape[0]
    d_lru = lru_conv_w.shape[2]
    widths = (d_lru, w_br_fourier.shape[1], w_br_pool.shape[1], w_br_sconv.shape[1])
    n_small = 2 * widths[0] + widths[1] + widths[2] + 3 * widths[3]
    assert bn % LRU_BATCH == 0 and w_in.shape[2] == n_small + N_BRANCH * d

    pad = (-(bn + 1)) % V7X_SUBLANES
    c_all = jnp.concatenate([c, c_ctx[None, :], jnp.zeros((pad, d), F32)], axis=0)
    mod_all = _mod_call(c_all, w_mod, b_mod).reshape(depth, bn + 1 + pad, N_MOD, 1, d)

    row = lambda v: v.reshape(1, -1)
    g_fin = row(g_final)
    zeros_state = jnp.zeros((bn, d_lru), F32)

    for l in range(depth):
        mx = [mod_all[l, :bn, i] for i in range(N_MOD)]
        mc = [jnp.broadcast_to(mod_all[l, bn:bn + 1, i], (bn, 1, d)) for i in range(N_MOD)]
        w_small = w_in[l][:, :n_small].astype(BF16)
        w_gate = w_in[l][:, n_small:].astype(BF16)
        wd = [_lru_gate_weights(lru_w_a[l, dr], lru_w_x[l, dr]) for dr in range(2)]
        bd = [0.5 * jnp.concatenate([lru_b_a[l, dr], lru_b_x[l, dr]])[None, :] for dr in range(2)]
        lam = [row(lru_lam[l, dr]) for dr in range(2)]
        pw = _block_diag(pool_w[l]).astype(BF16)
        branch_w = [w.astype(BF16) for w in (w_br_lru[l], w_br_fourier[l], w_br_pool[l], w_br_sconv[l])]
        wo = w_out[l].astype(BF16)
        w1 = w_ff1[l].astype(BF16)
        w2 = w_ff2[l].astype(BF16)
        g1, g2 = row(g_norm1[l]), row(g_norm2[l])
        cw, cb = lru_conv_w[l], row(lru_conv_b[l])
        last = l == depth - 1

        def mixer(s, m, row_len, h0):
            xa, gz, zf, zp, gb, v = _inproj_call(s, m[0], m[1], g1, w_small, cw, cb, widths)
            hf, hf_last = _lru_call(0, xa, wd[0], bd[0], lam[0], h0[0])
            y_lru, hb_last = _lru_call(1, xa, wd[1], bd[1], lam[1], h0[1], hf=hf, gz=gz)
            y_f = _fourier_call(zf)
            y_p, y_s = _poolconv_call(zp, gb, v, row_len, pw, row(pool_scale[l]), sconv_w[l])
            s = _merge_call(s, y_lru, y_f, y_p, y_s, m[0], m[1], m[2], g1, w_gate, *branch_w, wo)
            return s, (hf_last, hb_last)

        if not last:
            ctx, h_ctx = mixer(ctx, mc, ctx.shape[1], (zeros_state, zeros_state))
            ctx = _mlp_call(ctx, mc[3], mc[4], mc[5], g2, w1, w2, g_fin, False)
        else:
            (xa_c,) = _inproj_call(ctx, mc[0], mc[1], g1, w_small[:, :d_lru], cw, cb)
            h_ctx = tuple(_lru_call(dr, xa_c, wd[dr], bd[dr], lam[dr], zeros_state, emit=False)[0] for dr in range(2))
        x, _ = mixer(x, mx, GRID_W, h_ctx)
        x = _mlp_call(x, mx[3], mx[4], mx[5], g2, w1, w2, g_fin, last)
    return x
```

```python
import functools

import numpy as np
import jax
import jax.numpy as jnp
from jax import lax
from jax.experimental import pallas as pl
from jax.experimental.pallas import tpu as pltpu

F32 = jnp.float32
BF16 = jnp.bfloat16

EPS = 1e-6
LRU_C = 8.0
LRU_CONV = 4
LRU_CONV_LEFT = 2
POOL_WINDOWS = (2, 4, 8, 16)
N_BRANCH = 4
N_MOD = 6
GRID_W = 64
LOG2_E = 1.4426950408889634
SQRT_GUARD = 1e-30

V7X_LANES = 128
V7X_SUBLANES = 8
V7X_MXU_DIM = 256
V7X_VMEM_LIMIT_BYTES = 56 * 1024 * 1024

INPROJ_TILE = 1024
MERGE_TILE = 1024
MLP_TILE = 1024
POOL_TILE = 256
POOL_UNROLL = 4
LRU_TILE = 256
LRU_BATCH = V7X_SUBLANES
LRU_PREP_UNROLL = 4
SCAN_UNROLL = 8
DFT_RADIX = 4


def _params(sem, vmem=V7X_VMEM_LIMIT_BYTES):
    return pltpu.CompilerParams(dimension_semantics=sem, vmem_limit_bytes=vmem)


def _const_spec(shape):
    nd = len(shape)
    return pl.BlockSpec(shape, lambda *_: (0,) * nd, pipeline_mode=pl.Buffered(1))


def _norm_mod(x, g, shift, scale):
    ms = jnp.mean(x * x, axis=-1, keepdims=True)
    y = (x * lax.rsqrt(ms + EPS)) * g
    return y * (1.0 + scale) + shift


def _dot(a, b):
    return jnp.dot(a, b, preferred_element_type=F32)


def _mod_body(c_ref, w_ref, b_ref, o_ref):
    c = c_ref[...]
    s = c * jax.nn.sigmoid(c)
    o_ref[0] = _dot(s.astype(BF16), w_ref[0].astype(BF16)) + b_ref[0]


def _mod_call(c_all, w_mod, b_mod):
    depth, d, n = w_mod.shape
    rows = c_all.shape[0]
    tn = n // 4
    return pl.pallas_call(
        _mod_body,
        grid=(depth, n // tn),
        in_specs=[
            pl.BlockSpec((rows, d), lambda l, j: (0, 0)),
            pl.BlockSpec((1, d, tn), lambda l, j: (l, 0, j)),
            pl.BlockSpec((1, 1, tn), lambda l, j: (l, 0, j)),
        ],
        out_specs=pl.BlockSpec((1, rows, tn), lambda l, j: (l, 0, j)),
        out_shape=jax.ShapeDtypeStruct((depth, rows, n), F32),
        compiler_params=_params(("parallel", "parallel")),
        name="adaln_mod",
    )(c_all, w_mod, b_mod.reshape(depth, 1, n))


def _shift_rows(z, prev, nxt, k, row8):
    tl = z.shape[0]
    if k < 0:
        r = pltpu.roll(z, -k, axis=0)
        head = jnp.where(row8 < -k, pltpu.roll(prev, -k, axis=0), r[:V7X_SUBLANES])
        return jnp.concatenate([head, r[V7X_SUBLANES:]], axis=0)
    r = pltpu.roll(z, tl - k, axis=0)
    tail = jnp.where(row8 >= V7X_SUBLANES - k, pltpu.roll(nxt, V7X_SUBLANES - k, axis=0), r[tl - V7X_SUBLANES:])
    return jnp.concatenate([r[:tl - V7X_SUBLANES], tail], axis=0)


def _inproj_body(nt, lru_only, x_ref, xp_ref, xn_ref, sh_ref, sc_ref, g_ref, w_ref, cw_ref, cb_ref, xa_ref, *o_refs):
    j = pl.program_id(1)
    tl = x_ref.shape[1]
    dl = xa_ref.shape[2]
    g, sh, sc = g_ref[...], sh_ref[0], sc_ref[0]
    rows = jnp.concatenate([x_ref[0], xp_ref[0], xn_ref[0]], axis=0)
    u_all = _norm_mod(rows, g, sh, sc).astype(BF16)
    z_all = _dot(u_all, w_ref[:, 0:dl])
    z = z_all[:tl]
    prev = jnp.where(j > 0, z_all[tl:tl + V7X_SUBLANES], 0.0)
    nxt = jnp.where(j < nt - 1, z_all[tl + V7X_SUBLANES:], 0.0)
    row8 = lax.broadcasted_iota(jnp.int32, (V7X_SUBLANES, dl), 0)
    xa = cb_ref[...] + cw_ref[LRU_CONV_LEFT:LRU_CONV_LEFT + 1, :] * z
    for tap in range(LRU_CONV):
        k = tap - LRU_CONV_LEFT
        if k != 0:
            xa = xa + cw_ref[tap:tap + 1, :] * _shift_rows(z, prev, nxt, k, row8)
    xa_ref[0] = xa
    if lru_only:
        return
    gz_ref, zf_ref, zp_ref, gb_ref, v_ref = o_refs
    u = u_all[:tl]
    off = dl
    gz_ref[0] = jax.nn.gelu(_dot(u, w_ref[:, off:off + dl]), approximate=True)
    off += dl
    for o_ref in (zf_ref, zp_ref):
        wd = o_ref.shape[2]
        o_ref[0] = _dot(u, w_ref[:, off:off + wd]).astype(o_ref.dtype)
        off += wd
    ds = gb_ref.shape[2]
    zs = _dot(u, w_ref[:, off:off + 3 * ds])
    gb_ref[0] = zs[:, 0:ds]
    v_ref[0] = zs[:, ds:2 * ds] * zs[:, 2 * ds:3 * ds]


def _inproj_call(x, shift, scale, g, w, conv_w, conv_b, widths=None):
    b, l, d = x.shape
    tl = min(l, INPROJ_TILE)
    nt = l // tl
    dl = conv_w.shape[1]
    tiles8 = tl // V7X_SUBLANES
    vec = pl.BlockSpec((1, 1, d), lambda i, j: (i, 0, 0))
    prev = pl.BlockSpec((1, V7X_SUBLANES, d), lambda i, j: (i, jnp.maximum(j * tiles8 - 1, 0), 0))
    nxt = pl.BlockSpec((1, V7X_SUBLANES, d), lambda i, j: (i, jnp.minimum((j + 1) * tiles8, l // V7X_SUBLANES - 1), 0))
    outs = [(dl, F32)]
    if widths is not None:
        _, cf, dp, ds = widths
        outs += [(dl, F32), (cf, BF16), (dp, F32), (ds, F32), (ds, F32)]
    return pl.pallas_call(
        functools.partial(_inproj_body, nt, widths is None),
        grid=(b, nt),
        in_specs=[
            pl.BlockSpec((1, tl, d), lambda i, j: (i, j, 0)), prev, nxt,
            vec, vec,
            _const_spec((1, d)),
            _const_spec(w.shape), _const_spec(conv_w.shape), _const_spec((1, dl)),
        ],
        out_specs=[pl.BlockSpec((1, tl, wd), lambda i, j: (i, j, 0)) for wd, _ in outs],
        out_shape=[jax.ShapeDtypeStruct((b, l, wd), dt) for wd, dt in outs],
        compiler_params=_params(("parallel", "parallel")),
        name="in_proj",
    )(x, x, x, shift, scale, g, w, conv_w, conv_b)


def _lru_slab_pitch(tl):
    return tl * LRU_BATCH + V7X_SUBLANES


def _lru_body(direction, emit, *refs):
    if not emit:
        (xa_ref, wd_ref, bd_ref, lam_ref, h0_ref, hl_ref, ab_s, h_s) = refs
    elif direction == 0:
        (xa_ref, wd_ref, bd_ref, lam_ref, h0_ref, out_ref, hl_ref, ab_s, h_s) = refs
    else:
        (xa_ref, wd_ref, bd_ref, lam_ref, h0_ref, hf_ref, gz_ref, out_ref, hl_ref, ab_s, h_s) = refs
    nb, tl, dl = xa_ref.shape
    nslab = dl // V7X_LANES
    pitch = _lru_slab_pitch(tl)

    def slab(kind, s, start, size, stride=None):
        return pl.ds((2 * s + kind) * pitch + start, size, stride=stride)

    @pl.when(pl.program_id(1) == 0)
    def _():
        h_s[...] = h0_ref[...]

    half_rate = (-0.5 * LRU_C * LOG2_E) * jax.nn.softplus(-lam_ref[...])
    ngroups = wd_ref.shape[0]
    gw = dl // ngroups

    def prep(bi, carry):
        xa = xa_ref[bi]
        xb = xa.astype(BF16)
        parts = [_dot(xb[:, g * gw:(g + 1) * gw], wd_ref[g]) for g in range(ngroups)]
        ta = jnp.tanh(jnp.concatenate([p[:, :gw] for p in parts], axis=1) + bd_ref[:, :dl])
        tx = jnp.tanh(jnp.concatenate([p[:, gw:] for p in parts], axis=1) + bd_ref[:, dl:])
        a = jnp.exp2(half_rate * ta + half_rate)
        y = 1.0 - a * a
        hx = 0.5 * xa
        bb = (y * lax.rsqrt(jnp.maximum(y, SQRT_GUARD))) * (hx * tx + hx)
        for s in range(nslab):
            lanes = slice(s * V7X_LANES, (s + 1) * V7X_LANES)
            ab_s[slab(0, s, bi, tl, LRU_BATCH), :] = a[:, lanes]
            ab_s[slab(1, s, bi, tl, LRU_BATCH), :] = bb[:, lanes]
        return carry

    lax.fori_loop(0, nb, prep, 0, unroll=LRU_PREP_UNROLL)

    def steps(g, hs):
        first = g * SCAN_UNROLL if direction == 0 else tl - SCAN_UNROLL - g * SCAN_UNROLL
        base = pl.multiple_of(first * LRU_BATCH, SCAN_UNROLL * LRU_BATCH)
        for k in range(SCAN_UNROLL):
            t = k if direction == 0 else SCAN_UNROLL - 1 - k
            row = base + t * LRU_BATCH
            new = []
            for s in range(nslab):
                h = ab_s[slab(0, s, row, LRU_BATCH), :] * hs[s] + ab_s[slab(1, s, row, LRU_BATCH), :]
                ab_s[slab(1, s, row, LRU_BATCH), :] = h
                new.append(h)
            hs = tuple(new)
        return hs

    h_init = tuple(h_s[:, s * V7X_LANES:(s + 1) * V7X_LANES] for s in range(nslab))
    h_fin = lax.fori_loop(0, tl // SCAN_UNROLL, steps, h_init)
    h_last = jnp.concatenate(h_fin, axis=-1)
    h_s[...] = h_last
    hl_ref[...] = h_last

    if emit:
        def write(bi, carry):
            h = jnp.concatenate([ab_s[slab(1, s, bi, tl, LRU_BATCH), :] for s in range(nslab)], axis=-1)
            if direction == 0:
                out_ref[bi] = h
            else:
                out_ref[bi] = ((hf_ref[bi] + h) * gz_ref[bi]).astype(out_ref.dtype)
            return carry

        lax.fori_loop(0, nb, write, 0)


def _lru_call(direction, xa, wd, bd, lam, h0, hf=None, gz=None, emit=True):
    b, l, dl = xa.shape
    tl = min(l, LRU_TILE)
    nt = l // tl
    nb = LRU_BATCH
    main = pl.BlockSpec((nb, tl, dl), (lambda g, j: (g, j, 0)) if direction == 0 else (lambda g, j: (g, nt - 1 - j, 0)))
    state = pl.BlockSpec((nb, dl), lambda g, j: (g, 0))
    in_specs = [main, _const_spec(wd.shape), _const_spec((1, 2 * dl)), _const_spec((1, dl)), state]
    args = [xa, wd, bd, lam, h0]
    state_shape = jax.ShapeDtypeStruct((b, dl), F32)
    out_specs, out_shape = [state], [state_shape]
    if emit:
        out_specs, out_shape = [main, state], [jax.ShapeDtypeStruct((b, l, dl), F32 if direction == 0 else BF16), state_shape]
        if direction == 1:
            in_specs += [main, main]
            args += [hf, gz]
    slabs = pltpu.VMEM((2 * (dl // V7X_LANES) * _lru_slab_pitch(tl), V7X_LANES), F32)
    return pl.pallas_call(
        functools.partial(_lru_body, direction, emit),
        grid=(b // nb, nt),
        in_specs=in_specs,
        out_specs=out_specs,
        out_shape=out_shape,
        scratch_shapes=[slabs, pltpu.VMEM((nb, dl), F32)],
        compiler_params=_params(("parallel", "arbitrary")),
        name=f"rglru_dir{direction}",
    )(*args)


def _fourier_body(scale, zf_ref, wc_ref, tab_ref, o_ref, y_s):
    l, cf = zf_ref.shape[1], zf_ref.shape[2]
    lq = l // DFT_RADIX
    ab = _dot(zf_ref[0], wc_ref[...])
    a0, a1, a2, a3 = (ab[q * lq:(q + 1) * lq, :cf] for q in range(DFT_RADIX))
    b0, b1, b2, b3 = (ab[q * lq:(q + 1) * lq, cf:] for q in range(DFT_RADIX))
    g_re = (a0 + a1 + a2 + a3, a0 - b1 - a2 + b3, a0 - a1 + a2 - a3, a0 + b1 - a2 - b3)
    g_im = (-(b0 + b1 + b2 + b3), a3 + b2 - a1 - b0, b1 + b3 - b0 - b2, a1 + b2 - a3 - b0)
    for r in range(DFT_RADIX):
        g = jnp.concatenate([g_re[r], g_im[r]], axis=0).astype(BF16)
        yr = _dot(tab_ref[r], g) * scale
        for s in range(cf // V7X_LANES):
            y_s[s, pl.ds(r, lq, stride=DFT_RADIX), :] = yr[:, s * V7X_LANES:(s + 1) * V7X_LANES]
    o_ref[0] = jnp.concatenate([y_s[s] for s in range(cf // V7X_LANES)], axis=-1).astype(o_ref.dtype)


def _dft_tables(l, cf):
    lq = l // DFT_RADIX
    m = np.arange(lq)
    tabs = []
    for r in range(DFT_RADIX):
        k = DFT_RADIX * np.arange(lq) + r
        ang = 2.0 * np.pi * ((k[:, None] * m[None, :]) % l) / l
        tabs.append(np.concatenate([np.cos(ang), np.sin(ang)], axis=1))
    c = np.arange(cf)
    ang_c = 2.0 * np.pi * ((c[:, None] * c[None, :]) % cf) / cf
    wc = np.concatenate([np.cos(ang_c), np.sin(ang_c)], axis=1)
    return jnp.asarray(np.stack(tabs).astype(np.float32)).astype(BF16), jnp.asarray(wc.astype(np.float32)).astype(BF16)


def _fourier_call(zf):
    b, l, cf = zf.shape
    tabs, wc = _dft_tables(l, cf)
    blk = pl.BlockSpec((1, l, cf), lambda i: (i, 0, 0))
    return pl.pallas_call(
        functools.partial(_fourier_body, float(1.0 / np.sqrt(l * cf))),
        grid=(b,),
        in_specs=[blk, _const_spec(wc.shape), _const_spec(tabs.shape)],
        out_specs=blk,
        out_shape=jax.ShapeDtypeStruct((b, l, cf), BF16),
        scratch_shapes=[pltpu.VMEM((cf // V7X_LANES, l, V7X_LANES), F32)],
        compiler_params=_params(("parallel",)),
        name="fourier_mix",
    )(zf, wc, tabs)


def _poolconv_body(zp_ref, gb_ref, v_ref, pm_ref, ic_ref, pw_ref, ps_ref, sw_ref, yp_ref, ys_ref):
    l, dp = zp_ref.shape[1], zp_ref.shape[2]
    gw = dp // pm_ref.shape[0]
    per_slab = V7X_LANES // gw
    lane_group = lax.broadcasted_iota(jnp.int32, (POOL_TILE, V7X_LANES), 1) // gw

    def pool_tile(i, carry):
        rows = pl.ds(pl.multiple_of(i * POOL_TILE, POOL_TILE), POOL_TILE)
        u = zp_ref[0, rows, :]
        hi = u.astype(BF16)
        lo = (u - hi.astype(F32)).astype(BF16)
        slabs = []
        for s in range(dp // V7X_LANES):
            lanes = slice(s * V7X_LANES, (s + 1) * V7X_LANES)
            both = jnp.concatenate([hi[:, lanes], lo[:, lanes]], axis=1)
            acc = None
            for k in range(per_slab):
                rr = _dot(pm_ref[s * per_slab + k], both)
                sg = rr[:, :V7X_LANES] + rr[:, V7X_LANES:]
                acc = sg if acc is None else jnp.where(lane_group == k, sg, acc)
            slabs.append(acc)
        p = jnp.concatenate(slabs, axis=1) * ic_ref[...] - u
        yp_ref[0, rows, :] = (_dot(p.astype(BF16), pw_ref[...]) * ps_ref[...]).astype(yp_ref.dtype)
        return carry

    lax.fori_loop(0, l // POOL_TILE, pool_tile, 0, unroll=min(POOL_UNROLL, l // POOL_TILE))

    v = v_ref[0]
    row8 = lax.broadcasted_iota(jnp.int32, (V7X_SUBLANES, v.shape[1]), 0)
    zeros8 = jnp.zeros((V7X_SUBLANES, v.shape[1]), F32)
    conv = (sw_ref[0:1, :] * _shift_rows(v, zeros8, zeros8, -1, row8) + sw_ref[1:2, :] * v
            + sw_ref[2:3, :] * _shift_rows(v, zeros8, zeros8, 1, row8))
    ys_ref[0] = (gb_ref[0] * conv).astype(ys_ref.dtype)


def _pool_tables(row_len, dp):
    t = np.arange(POOL_TILE)
    r0 = (t // row_len) * row_len
    mats, inv = [], []
    for w in POOL_WINDOWS:
        lo = np.maximum(t - w // 2, r0)
        hi = np.minimum(t + w // 2, r0 + row_len)
        mats.append(((t[None, :] >= lo[:, None]) & (t[None, :] < hi[:, None])).astype(np.float32))
        inv.append(np.repeat((1.0 / (hi - lo))[:, None], dp // len(POOL_WINDOWS), axis=1))
    pm = jnp.asarray(np.stack(mats)).astype(BF16)
    ic = jnp.asarray(np.concatenate(inv, axis=1).astype(np.float32))
    return pm, ic


def _poolconv_call(zp, gb, v, row_len, pw, pscale, sw):
    b, l, dp = zp.shape
    assert row_len <= POOL_TILE and POOL_TILE % row_len == 0 and l % POOL_TILE == 0
    assert V7X_LANES % (dp // len(POOL_WINDOWS)) == 0
    pm, ic = _pool_tables(row_len, dp)
    ds = gb.shape[2]

    def seq(width):
        return pl.BlockSpec((1, l, width), lambda i: (i, 0, 0))

    return pl.pallas_call(
        _poolconv_body,
        grid=(b,),
        in_specs=[seq(dp), seq(ds), seq(ds), _const_spec(pm.shape), _const_spec(ic.shape), _const_spec(pw.shape),
                  _const_spec((1, dp)), _const_spec(sw.shape)],
        out_specs=[seq(dp), seq(ds)],
        out_shape=[jax.ShapeDtypeStruct((b, l, dp), BF16), jax.ShapeDtypeStruct((b, l, ds), BF16)],
        compiler_params=_params(("parallel",)),
        name="pool_sconv",
    )(zp, gb, v, pm, ic, pw, pscale, sw)


def _merge_body(x_ref, yl_ref, yf_ref, yp_ref, ys_ref, sh_ref, sc_ref, gt_ref, g_ref,
                wg_ref, wl_ref, wf_ref, wp_ref, ws_ref, wo_ref, o_ref):
    x = x_ref[0]
    d = x.shape[-1]
    u = _norm_mod(x, g_ref[...], sh_ref[0], sc_ref[0]).astype(BF16)
    twice = None
    for i, (y_ref, w_ref) in enumerate(((yl_ref, wl_ref), (yf_ref, wf_ref), (yp_ref, wp_ref), (ys_ref, ws_ref))):
        t = jnp.tanh(_dot(u, wg_ref[:, i * d:(i + 1) * d]))
        p = _dot(y_ref[0], w_ref[...])
        term = t * p + p
        twice = term if twice is None else twice + term
    o_ref[0] = x + gt_ref[0] * _dot((0.5 * twice).astype(BF16), wo_ref[...])


def _merge_call(x, yl, yf, yp, ys, shift, scale, gate, g, wg, wl, wf, wp, ws, wo):
    b, l, d = x.shape
    tm = min(l, MERGE_TILE)
    vec = pl.BlockSpec((1, 1, d), lambda i, j: (i, 0, 0))

    def tok(a):
        return pl.BlockSpec((1, tm, a.shape[2]), lambda i, j: (i, j, 0))

    return pl.pallas_call(
        _merge_body,
        grid=(b, l // tm),
        in_specs=[tok(x), tok(yl), tok(yf), tok(yp), tok(ys), vec, vec, vec, _const_spec((1, d))]
        + [_const_spec(w.shape) for w in (wg, wl, wf, wp, ws, wo)],
        out_specs=tok(x),
        out_shape=jax.ShapeDtypeStruct(x.shape, F32),
        compiler_params=_params(("parallel", "parallel")),
        name="merge_out",
    )(x, yl, yf, yp, ys, shift, scale, gate, g, wg, wl, wf, wp, ws, wo)


def _mlp_body(final_norm, x_ref, sh_ref, sc_ref, gt_ref, g_ref, w1_ref, w2_ref, gf_ref, o_ref):
    x = x_ref[0]
    d = x.shape[-1]
    dff = w1_ref.shape[1]
    u = _norm_mod(x, g_ref[...], sh_ref[0], sc_ref[0]).astype(BF16)
    acc = None
    for k in range(dff // d):
        cols = slice(k * d, (k + 1) * d)
        h = jnp.square(jnp.maximum(_dot(u, w1_ref[:, cols]), 0.0)).astype(BF16)
        part = _dot(h, w2_ref[cols, :])
        acc = part if acc is None else acc + part
    y = x + gt_ref[0] * acc
    if final_norm:
        ms = jnp.mean(y * y, axis=-1, keepdims=True)
        y = (y * lax.rsqrt(ms + EPS)) * gf_ref[...]
    o_ref[0] = y


def _mlp_call(x, shift, scale, gate, g, w1, w2, g_final, final_norm):
    b, l, d = x.shape
    tm = min(l, MLP_TILE)
    vec = pl.BlockSpec((1, 1, d), lambda i, j: (i, 0, 0))
    tok = pl.BlockSpec((1, tm, d), lambda i, j: (i, j, 0))
    return pl.pallas_call(
        functools.partial(_mlp_body, final_norm),
        grid=(b, l // tm),
        in_specs=[tok, vec, vec, vec, _const_spec((1, d)), _const_spec(w1.shape), _const_spec(w2.shape),
                  _const_spec((1, d))],
        out_specs=tok,
        out_shape=jax.ShapeDtypeStruct(x.shape, F32),
        compiler_params=_params(("parallel", "parallel")),
        name="sqrelu_mlp",
    )(x, shift, scale, gate, g, w1, w2, g_final)


def _block_diag(w):
    h, bs, _ = w.shape
    eye = jnp.eye(h, dtype=w.dtype)
    return (eye[:, None, :, None] * w[:, :, None, :]).reshape(h * bs, h * bs)


def _lru_gate_weights(w_a, w_x):
    h, bs, _ = w_a.shape
    per = max(V7X_MXU_DIM // bs, 1)
    groups = [jnp.concatenate([_block_diag(w_a[g:g + per]), _block_diag(w_x[g:g + per])], axis=1)
              for g in range(0, h, per)]
    return (0.5 * jnp.stack(groups)).astype(BF16)


def kernel(x, c, ctx, c_ctx, w_mod, b_mod, g_norm1, g_norm2, w_in, lru_conv_w, lru_conv_b, lru_w_a, lru_b_a, lru_w_x, lru_b_x, lru_lam, pool_w, pool_scale, sconv_w, w_br_lru, w_br_fourier, w_br_pool, w_br_sconv, w_out, w_ff1, w_ff2, g_final):
    bn, seq, d = x.shape
    depth = w_mod.shape[0]
    d_lru = lru_conv_w.shape[2]
    widths = (d_lru, w_br_fourier.shape[1], w_br_pool.shape[1], w_br_sconv.shape[1])
    n_small = 2 * widths[0] + widths[1] + widths[2] + 3 * widths[3]
    assert bn % LRU_BATCH == 0 and w_in.shape[2] == n_small + N_BRANCH * d

    pad = (-(bn + 1)) % V7X_SUBLANES
    c_all = jnp.concatenate([c, c_ctx[None, :], jnp.zeros((pad, d), F32)], axis=0)
    mod_all = _mod_call(c_all, w_mod, b_mod).reshape(depth, bn + 1 + pad, N_MOD, 1, d)

    row = lambda v: v.reshape(1, -1)
    g_fin = row(g_final)
    zeros_state = jnp.zeros((bn, d_lru), F32)

    for l in range(depth):
        mx = [mod_all[l, :bn, i] for i in range(N_MOD)]
        mc = [jnp.broadcast_to(mod_all[l, bn:bn + 1, i], (bn, 1, d)) for i in range(N_MOD)]
        w_small = w_in[l][:, :n_small].astype(BF16)
        w_gate = (0.5 * w_in[l][:, n_small:]).astype(BF16)
        wd = [_lru_gate_weights(lru_w_a[l, dr], lru_w_x[l, dr]) for dr in range(2)]
        bd = [0.5 * jnp.concatenate([lru_b_a[l, dr], lru_b_x[l, dr]])[None, :] for dr in range(2)]
        lam = [row(lru_lam[l, dr]) for dr in range(2)]
        pw = _block_diag(pool_w[l]).astype(BF16)
        branch_w = [w.astype(BF16) for w in (w_br_lru[l], w_br_fourier[l], w_br_pool[l], w_br_sconv[l])]
        wo = w_out[l].astype(BF16)
        w1 = w_ff1[l].astype(BF16)
        w2 = w_ff2[l].astype(BF16)
        g1, g2 = row(g_norm1[l]), row(g_norm2[l])
        cw, cb = lru_conv_w[l], row(lru_conv_b[l])
        last = l == depth - 1

        def mixer(s, m, row_len, h0):
            xa, gz, zf, zp, gb, v = _inproj_call(s, m[0], m[1], g1, w_small, cw, cb, widths)
            hf, hf_last = _lru_call(0, xa, wd[0], bd[0], lam[0], h0[0])
            y_lru, hb_last = _lru_call(1, xa, wd[1], bd[1], lam[1], h0[1], hf=hf, gz=gz)
            y_f = _fourier_call(zf)
            y_p, y_s = _poolconv_call(zp, gb, v, row_len, pw, row(pool_scale[l]), sconv_w[l])
            s = _merge_call(s, y_lru, y_f, y_p, y_s, m[0], m[1], m[2], g1, w_gate, *branch_w, wo)
            return s, (hf_last, hb_last)

        if not last:
            ctx, h_ctx = mixer(ctx, mc, ctx.shape[1], (zeros_state, zeros_state))
            ctx = _mlp_call(ctx, mc[3], mc[4], mc[5], g2, w1, w2, g_fin, False)
        else:
            (xa_c,) = _inproj_call(ctx, mc[0], mc[1], g1, w_small[:, :d_lru], cw, cb)
            h_ctx = tuple(_lru_call(dr, xa_c, wd[dr], bd[dr], lam[dr], zeros_state, emit=False)[0] for dr in range(2))
        x, _ = mixer(x, mx, GRID_W, h_ctx)
        x = _mlp_call(x, mx[3], mx[4], mx[5], g2, w1, w2, g_fin, last)
    return x
```

```python
import functools

import numpy as np
import jax
import jax.numpy as jnp
from jax import lax
from jax.experimental import pallas as pl
from jax.experimental.pallas import tpu as pltpu

F32 = jnp.float32
BF16 = jnp.bfloat16

EPS = 1e-6
LRU_C = 8.0
LRU_CONV = 4
LRU_CONV_LEFT = 2
POOL_WINDOWS = (2, 4, 8, 16)
N_BRANCH = 4
N_MOD = 6
GRID_W = 64
LOG2_E = 1.4426950408889634
SQRT_GUARD = 1e-30

V7X_LANES = 128
V7X_SUBLANES = 8
V7X_MXU_DIM = 256
V7X_VMEM_LIMIT_BYTES = 56 * 1024 * 1024

INPROJ_TILE = 1024
MERGE_TILE = 1024
MLP_TILE = 1024
POOL_TILE = 256
POOL_UNROLL = 4
LRU_TILE = 256
LRU_BATCH = V7X_SUBLANES
LRU_PREP_UNROLL = 8
SCAN_UNROLL = 8
DFT_RADIX = 4


def _params(sem, vmem=V7X_VMEM_LIMIT_BYTES):
    return pltpu.CompilerParams(dimension_semantics=sem, vmem_limit_bytes=vmem)


def _const_spec(shape):
    nd = len(shape)
    return pl.BlockSpec(shape, lambda *_: (0,) * nd, pipeline_mode=pl.Buffered(1))


def _norm_mod(x, g, shift, scale):
    ms = jnp.mean(x * x, axis=-1, keepdims=True)
    y = (x * lax.rsqrt(ms + EPS)) * g
    return y * (1.0 + scale) + shift


def _dot(a, b):
    return jnp.dot(a, b, preferred_element_type=F32)


def _mod_body(c_ref, w_ref, b_ref, o_ref):
    c = c_ref[...]
    s = c * jax.nn.sigmoid(c)
    o_ref[0] = _dot(s.astype(BF16), w_ref[0].astype(BF16)) + b_ref[0]


def _mod_call(c_all, w_mod, b_mod):
    depth, d, n = w_mod.shape
    rows = c_all.shape[0]
    tn = n // 4
    return pl.pallas_call(
        _mod_body,
        grid=(depth, n // tn),
        in_specs=[
            pl.BlockSpec((rows, d), lambda l, j: (0, 0)),
            pl.BlockSpec((1, d, tn), lambda l, j: (l, 0, j)),
            pl.BlockSpec((1, 1, tn), lambda l, j: (l, 0, j)),
        ],
        out_specs=pl.BlockSpec((1, rows, tn), lambda l, j: (l, 0, j)),
        out_shape=jax.ShapeDtypeStruct((depth, rows, n), F32),
        compiler_params=_params(("parallel", "parallel")),
        name="adaln_mod",
    )(c_all, w_mod, b_mod.reshape(depth, 1, n))


def _shift_rows(z, prev, nxt, k, row8):
    tl = z.shape[0]
    if k < 0:
        r = pltpu.roll(z, -k, axis=0)
        head = jnp.where(row8 < -k, pltpu.roll(prev, -k, axis=0), r[:V7X_SUBLANES])
        return jnp.concatenate([head, r[V7X_SUBLANES:]], axis=0)
    r = pltpu.roll(z, tl - k, axis=0)
    tail = jnp.where(row8 >= V7X_SUBLANES - k, pltpu.roll(nxt, V7X_SUBLANES - k, axis=0), r[tl - V7X_SUBLANES:])
    return jnp.concatenate([r[:tl - V7X_SUBLANES], tail], axis=0)


def _inproj_body(nt, lru_only, x_ref, xp_ref, xn_ref, sh_ref, sc_ref, g_ref, w_ref, cw_ref, cb_ref, xa_ref, xb_ref, *o_refs):
    j = pl.program_id(1)
    tl = x_ref.shape[1]
    dl = xa_ref.shape[2]
    g, sh, sc = g_ref[...], sh_ref[0], sc_ref[0]
    rows = jnp.concatenate([x_ref[0], xp_ref[0], xn_ref[0]], axis=0)
    u_all = _norm_mod(rows, g, sh, sc).astype(BF16)
    z_all = _dot(u_all, w_ref[:, 0:dl])
    z = z_all[:tl]
    prev = jnp.where(j > 0, z_all[tl:tl + V7X_SUBLANES], 0.0)
    nxt = jnp.where(j < nt - 1, z_all[tl + V7X_SUBLANES:], 0.0)
    row8 = lax.broadcasted_iota(jnp.int32, (V7X_SUBLANES, dl), 0)
    xa = cb_ref[...] + cw_ref[LRU_CONV_LEFT:LRU_CONV_LEFT + 1, :] * z
    for tap in range(LRU_CONV):
        k = tap - LRU_CONV_LEFT
        if k != 0:
            xa = xa + cw_ref[tap:tap + 1, :] * _shift_rows(z, prev, nxt, k, row8)
    xa_ref[0] = xa
    xb_ref[0] = xa.astype(BF16)
    if lru_only:
        return
    gz_ref, zf_ref, zp_ref, gb_ref, v_ref = o_refs
    u = u_all[:tl]
    off = dl
    gz_ref[0] = jax.nn.gelu(_dot(u, w_ref[:, off:off + dl]), approximate=True)
    off += dl
    for o_ref in (zf_ref, zp_ref):
        wd = o_ref.shape[2]
        o_ref[0] = _dot(u, w_ref[:, off:off + wd]).astype(o_ref.dtype)
        off += wd
    ds = gb_ref.shape[2]
    zs = _dot(u, w_ref[:, off:off + 3 * ds])
    gb_ref[0] = zs[:, 0:ds]
    v_ref[0] = zs[:, ds:2 * ds] * zs[:, 2 * ds:3 * ds]


def _inproj_call(x, shift, scale, g, w, conv_w, conv_b, widths=None):
    b, l, d = x.shape
    tl = min(l, INPROJ_TILE)
    nt = l // tl
    dl = conv_w.shape[1]
    tiles8 = tl // V7X_SUBLANES
    vec = pl.BlockSpec((1, 1, d), lambda i, j: (i, 0, 0))
    prev = pl.BlockSpec((1, V7X_SUBLANES, d), lambda i, j: (i, jnp.maximum(j * tiles8 - 1, 0), 0))
    nxt = pl.BlockSpec((1, V7X_SUBLANES, d), lambda i, j: (i, jnp.minimum((j + 1) * tiles8, l // V7X_SUBLANES - 1), 0))
    outs = [(dl, F32), (dl, BF16)]
    if widths is not None:
        _, cf, dp, ds = widths
        outs += [(dl, F32), (cf, BF16), (dp, F32), (ds, F32), (ds, F32)]
    return pl.pallas_call(
        functools.partial(_inproj_body, nt, widths is None),
        grid=(b, nt),
        in_specs=[
            pl.BlockSpec((1, tl, d), lambda i, j: (i, j, 0)), prev, nxt,
            vec, vec,
            _const_spec((1, d)),
            _const_spec(w.shape), _const_spec(conv_w.shape), _const_spec((1, dl)),
        ],
        out_specs=[pl.BlockSpec((1, tl, wd), lambda i, j: (i, j, 0)) for wd, _ in outs],
        out_shape=[jax.ShapeDtypeStruct((b, l, wd), dt) for wd, dt in outs],
        compiler_params=_params(("parallel", "parallel")),
        name="in_proj",
    )(x, x, x, shift, scale, g, w, conv_w, conv_b)


def _lru_slab_pitch(tl):
    return tl * LRU_BATCH + V7X_SUBLANES


def _lru_body(direction, emit, *refs):
    if not emit:
        (xa_ref, xb_ref, wd_ref, bd_ref, lam_ref, h0_ref, hl_ref, ab_s, h_s) = refs
    elif direction == 0:
        (xa_ref, xb_ref, wd_ref, bd_ref, lam_ref, h0_ref, out_ref, hl_ref, ab_s, h_s) = refs
    else:
        (xa_ref, xb_ref, wd_ref, bd_ref, lam_ref, h0_ref, hf_ref, gz_ref, out_ref, hl_ref, ab_s, h_s) = refs
    nb, tl, dl = xa_ref.shape
    nslab = dl // V7X_LANES
    pitch = _lru_slab_pitch(tl)
    h_to_out = emit and direction == 0

    def slab(kind, s, start, size, stride=None):
        return pl.ds((2 * s + kind) * pitch + start, size, stride=stride)

    @pl.when(pl.program_id(1) == 0)
    def _():
        h_s[...] = h0_ref[...]

    half_rate = (-0.5 * LRU_C * LOG2_E) * jax.nn.softplus(-lam_ref[...])
    ngroups = wd_ref.shape[0]
    gw = dl // ngroups

    def prep(bi, carry):
        xa = xa_ref[bi]
        xb = xb_ref[bi]
        parts = [_dot(xb[:, g * gw:(g + 1) * gw], wd_ref[g]) for g in range(ngroups)]
        ta = jnp.tanh(jnp.concatenate([p[:, :gw] for p in parts], axis=1) + bd_ref[:, :dl])
        tx = jnp.tanh(jnp.concatenate([p[:, gw:] for p in parts], axis=1) + bd_ref[:, dl:])
        a = jnp.exp2(half_rate * ta + half_rate)
        y = 1.0 - a * a
        hx = 0.5 * xa
        bb = (y * lax.rsqrt(jnp.maximum(y, SQRT_GUARD))) * (hx * tx + hx)
        for s in range(nslab):
            lanes = slice(s * V7X_LANES, (s + 1) * V7X_LANES)
            ab_s[slab(0, s, bi, tl, LRU_BATCH), :] = a[:, lanes]
            ab_s[slab(1, s, bi, tl, LRU_BATCH), :] = bb[:, lanes]
        return carry

    lax.fori_loop(0, nb, prep, 0, unroll=LRU_PREP_UNROLL)

    def steps(g, hs):
        first = g * SCAN_UNROLL if direction == 0 else tl - SCAN_UNROLL - g * SCAN_UNROLL
        base = pl.multiple_of(first * LRU_BATCH, SCAN_UNROLL * LRU_BATCH)
        for k in range(SCAN_UNROLL):
            t = k if direction == 0 else SCAN_UNROLL - 1 - k
            row = base + t * LRU_BATCH
            new = []
            for s in range(nslab):
                h = ab_s[slab(0, s, row, LRU_BATCH), :] * hs[s] + ab_s[slab(1, s, row, LRU_BATCH), :]
                if h_to_out:
                    out_ref[s, pl.ds(row, LRU_BATCH), :] = h
                elif emit:
                    ab_s[slab(1, s, row, LRU_BATCH), :] = h + hf_ref[s, pl.ds(row, LRU_BATCH), :]
                new.append(h)
            hs = tuple(new)
        return hs

    h_init = tuple(h_s[:, s * V7X_LANES:(s + 1) * V7X_LANES] for s in range(nslab))
    h_fin = lax.fori_loop(0, tl // SCAN_UNROLL, steps, h_init)
    h_last = jnp.concatenate(h_fin, axis=-1)
    h_s[...] = h_last
    hl_ref[...] = h_last

    if emit and direction == 1:
        def write(bi, carry):
            h = jnp.concatenate([ab_s[slab(1, s, bi, tl, LRU_BATCH), :] for s in range(nslab)], axis=-1)
            out_ref[bi] = (h * gz_ref[bi]).astype(out_ref.dtype)
            return carry

        lax.fori_loop(0, nb, write, 0)


def _lru_call(direction, xa, xb, wd, bd, lam, h0, hf=None, gz=None, emit=True):
    b, l, dl = xa.shape
    tl = min(l, LRU_TILE)
    nt = l // tl
    nb = LRU_BATCH
    nslab = dl // V7X_LANES
    tile = (lambda g, j: (g, j, 0)) if direction == 0 else (lambda g, j: (g, nt - 1 - j, 0))
    main = pl.BlockSpec((nb, tl, dl), tile)
    slab_block = (None, None, nslab, tl * nb, V7X_LANES)
    state = pl.BlockSpec((nb, dl), lambda g, j: (g, 0))
    in_specs = [main, main, _const_spec(wd.shape), _const_spec((1, 2 * dl)), _const_spec((1, dl)), state]
    args = [xa, xb, wd, bd, lam, h0]
    state_shape = jax.ShapeDtypeStruct((b, dl), F32)
    out_specs, out_shape = [state], [state_shape]
    if emit and direction == 0:
        out_specs = [pl.BlockSpec(slab_block, lambda g, j: (g, j, 0, 0, 0)), state]
        out_shape = [jax.ShapeDtypeStruct((b // nb, nt, nslab, tl * nb, V7X_LANES), F32), state_shape]
    elif emit:
        in_specs += [pl.BlockSpec(slab_block, lambda g, j: (g, nt - 1 - j, 0, 0, 0)), main]
        args += [hf, gz]
        out_specs, out_shape = [main, state], [jax.ShapeDtypeStruct((b, l, dl), BF16), state_shape]
    slabs = pltpu.VMEM((2 * nslab * _lru_slab_pitch(tl), V7X_LANES), F32)
    return pl.pallas_call(
        functools.partial(_lru_body, direction, emit),
        grid=(b // nb, nt),
        in_specs=in_specs,
        out_specs=out_specs,
        out_shape=out_shape,
        scratch_shapes=[slabs, pltpu.VMEM((nb, dl), F32)],
        compiler_params=_params(("parallel", "arbitrary")),
        name=f"rglru_dir{direction}",
    )(*args)


def _fourier_body(scale, zf_ref, wc_ref, tab_ref, o_ref, y_s):
    l, cf = zf_ref.shape[1], zf_ref.shape[2]
    lq = l // DFT_RADIX
    ab = _dot(zf_ref[0], wc_ref[...])
    a0, a1, a2, a3 = (ab[q * lq:(q + 1) * lq, :cf] for q in range(DFT_RADIX))
    b0, b1, b2, b3 = (ab[q * lq:(q + 1) * lq, cf:] for q in range(DFT_RADIX))
    g_re = (a0 + a1 + a2 + a3, a0 - b1 - a2 + b3, a0 - a1 + a2 - a3, a0 + b1 - a2 - b3)
    g_im = (-(b0 + b1 + b2 + b3), a3 + b2 - a1 - b0, b1 + b3 - b0 - b2, a1 + b2 - a3 - b0)
    for r in range(DFT_RADIX):
        g = jnp.concatenate([g_re[r], g_im[r]], axis=0).astype(BF16)
        yr = _dot(tab_ref[r], g) * scale
        for s in range(cf // V7X_LANES):
            y_s[s, pl.ds(r, lq, stride=DFT_RADIX), :] = yr[:, s * V7X_LANES:(s + 1) * V7X_LANES]
    o_ref[0] = jnp.concatenate([y_s[s] for s in range(cf // V7X_LANES)], axis=-1).astype(o_ref.dtype)


def _dft_tables(l, cf):
    lq = l // DFT_RADIX
    m = np.arange(lq)
    tabs = []
    for r in range(DFT_RADIX):
        k = DFT_RADIX * np.arange(lq) + r
        ang = 2.0 * np.pi * ((k[:, None] * m[None, :]) % l) / l
        tabs.append(np.concatenate([np.cos(ang), np.sin(ang)], axis=1))
    c = np.arange(cf)
    ang_c = 2.0 * np.pi * ((c[:, None] * c[None, :]) % cf) / cf
    wc = np.concatenate([np.cos(ang_c), np.sin(ang_c)], axis=1)
    return jnp.asarray(np.stack(tabs).astype(np.float32)).astype(BF16), jnp.asarray(wc.astype(np.float32)).astype(BF16)


def _fourier_call(zf):
    b, l, cf = zf.shape
    tabs, wc = _dft_tables(l, cf)
    blk = pl.BlockSpec((1, l, cf), lambda i: (i, 0, 0))
    return pl.pallas_call(
        functools.partial(_fourier_body, float(1.0 / np.sqrt(l * cf))),
        grid=(b,),
        in_specs=[blk, _const_spec(wc.shape), _const_spec(tabs.shape)],
        out_specs=blk,
        out_shape=jax.ShapeDtypeStruct((b, l, cf), BF16),
        scratch_shapes=[pltpu.VMEM((cf // V7X_LANES, l, V7X_LANES), F32)],
        compiler_params=_params(("parallel",)),
        name="fourier_mix",
    )(zf, wc, tabs)


def _poolconv_body(zp_ref, gb_ref, v_ref, pm_ref, ic_ref, pw_ref, ps_ref, sw_ref, yp_ref, ys_ref):
    l, dp = zp_ref.shape[1], zp_ref.shape[2]
    gw = dp // pm_ref.shape[0]
    per_slab = V7X_LANES // gw
    lane_group = lax.broadcasted_iota(jnp.int32, (POOL_TILE, V7X_LANES), 1) // gw

    def pool_tile(i, carry):
        rows = pl.ds(pl.multiple_of(i * POOL_TILE, POOL_TILE), POOL_TILE)
        u = zp_ref[0, rows, :]
        hi = u.astype(BF16)
        lo = (u - hi.astype(F32)).astype(BF16)
        slabs = []
        for s in range(dp // V7X_LANES):
            lanes = slice(s * V7X_LANES, (s + 1) * V7X_LANES)
            both = jnp.concatenate([hi[:, lanes], lo[:, lanes]], axis=1)
            acc = None
            for k in range(per_slab):
                rr = _dot(pm_ref[s * per_slab + k], both)
                sg = rr[:, :V7X_LANES] + rr[:, V7X_LANES:]
                acc = sg if acc is None else jnp.where(lane_group == k, sg, acc)
            slabs.append(acc)
        p = jnp.concatenate(slabs, axis=1) * ic_ref[...] - u
        yp_ref[0, rows, :] = (_dot(p.astype(BF16), pw_ref[...]) * ps_ref[...]).astype(yp_ref.dtype)
        return carry

    lax.fori_loop(0, l // POOL_TILE, pool_tile, 0, unroll=min(POOL_UNROLL, l // POOL_TILE))

    v = v_ref[0]
    row8 = lax.broadcasted_iota(jnp.int32, (V7X_SUBLANES, v.shape[1]), 0)
    zeros8 = jnp.zeros((V7X_SUBLANES, v.shape[1]), F32)
    conv = (sw_ref[0:1, :] * _shift_rows(v, zeros8, zeros8, -1, row8) + sw_ref[1:2, :] * v
            + sw_ref[2:3, :] * _shift_rows(v, zeros8, zeros8, 1, row8))
    ys_ref[0] = (gb_ref[0] * conv).astype(ys_ref.dtype)


def _pool_tables(row_len, dp):
    t = np.arange(POOL_TILE)
    r0 = (t // row_len) * row_len
    mats, inv = [], []
    for w in POOL_WINDOWS:
        lo = np.maximum(t - w // 2, r0)
        hi = np.minimum(t + w // 2, r0 + row_len)
        mats.append(((t[None, :] >= lo[:, None]) & (t[None, :] < hi[:, None])).astype(np.float32))
        inv.append(np.repeat((1.0 / (hi - lo))[:, None], dp // len(POOL_WINDOWS), axis=1))
    pm = jnp.asarray(np.stack(mats)).astype(BF16)
    ic = jnp.asarray(np.concatenate(inv, axis=1).astype(np.float32))
    return pm, ic


def _poolconv_call(zp, gb, v, row_len, pw, pscale, sw):
    b, l, dp = zp.shape
    assert row_len <= POOL_TILE and POOL_TILE % row_len == 0 and l % POOL_TILE == 0
    assert V7X_LANES % (dp // len(POOL_WINDOWS)) == 0
    pm, ic = _pool_tables(row_len, dp)
    ds = gb.shape[2]

    def seq(width):
        return pl.BlockSpec((1, l, width), lambda i: (i, 0, 0))

    return pl.pallas_call(
        _poolconv_body,
        grid=(b,),
        in_specs=[seq(dp), seq(ds), seq(ds), _const_spec(pm.shape), _const_spec(ic.shape), _const_spec(pw.shape),
                  _const_spec((1, dp)), _const_spec(sw.shape)],
        out_specs=[seq(dp), seq(ds)],
        out_shape=[jax.ShapeDtypeStruct((b, l, dp), BF16), jax.ShapeDtypeStruct((b, l, ds), BF16)],
        compiler_params=_params(("parallel",)),
        name="pool_sconv",
    )(zp, gb, v, pm, ic, pw, pscale, sw)


def _merge_body(x_ref, yl_ref, yf_ref, yp_ref, ys_ref, sh_ref, sc_ref, gt_ref, g_ref,
                wg_ref, wl_ref, wf_ref, wp_ref, ws_ref, wo_ref, o_ref):
    x = x_ref[0]
    d = x.shape[-1]
    u = _norm_mod(x, g_ref[...], sh_ref[0], sc_ref[0]).astype(BF16)
    twice = None
    for i, (y_ref, w_ref) in enumerate(((yl_ref, wl_ref), (yf_ref, wf_ref), (yp_ref, wp_ref), (ys_ref, ws_ref))):
        t = jnp.tanh(_dot(u, wg_ref[:, i * d:(i + 1) * d]))
        p = _dot(y_ref[0], w_ref[...])
        term = t * p + p
        twice = term if twice is None else twice + term
    o_ref[0] = x + gt_ref[0] * _dot((0.5 * twice).astype(BF16), wo_ref[...])


def _merge_call(x, yl, yf, yp, ys, shift, scale, gate, g, wg, wl, wf, wp, ws, wo):
    b, l, d = x.shape
    tm = min(l, MERGE_TILE)
    vec = pl.BlockSpec((1, 1, d), lambda i, j: (i, 0, 0))

    def tok(a):
        return pl.BlockSpec((1, tm, a.shape[2]), lambda i, j: (i, j, 0))

    return pl.pallas_call(
        _merge_body,
        grid=(b, l // tm),
        in_specs=[tok(x), tok(yl), tok(yf), tok(yp), tok(ys), vec, vec, vec, _const_spec((1, d))]
        + [_const_spec(w.shape) for w in (wg, wl, wf, wp, ws, wo)],
        out_specs=tok(x),
        out_shape=jax.ShapeDtypeStruct(x.shape, F32),
        compiler_params=_params(("parallel", "parallel")),
        name="merge_out",
    )(x, yl, yf, yp, ys, shift, scale, gate, g, wg, wl, wf, wp, ws, wo)


def _mlp_body(final_norm, x_ref, sh_ref, sc_ref, gt_ref, g_ref, w1_ref, w2_ref, gf_ref, o_ref):
    x = x_ref[0]
    d = x.shape[-1]
    dff = w1_ref.shape[1]
    u = _norm_mod(x, g_ref[...], sh_ref[0], sc_ref[0]).astype(BF16)
    acc = None
    for k in range(dff // d):
        cols = slice(k * d, (k + 1) * d)
        h = jnp.square(jnp.maximum(_dot(u, w1_ref[:, cols]), 0.0)).astype(BF16)
        part = _dot(h, w2_ref[cols, :])
        acc = part if acc is None else acc + part
    y = x + gt_ref[0] * acc
    if final_norm:
        ms = jnp.mean(y * y, axis=-1, keepdims=True)
        y = (y * lax.rsqrt(ms + EPS)) * gf_ref[...]
    o_ref[0] = y


def _mlp_call(x, shift, scale, gate, g, w1, w2, g_final, final_norm):
    b, l, d = x.shape
    tm = min(l, MLP_TILE)
    vec = pl.BlockSpec((1, 1, d), lambda i, j: (i, 0, 0))
    tok = pl.BlockSpec((1, tm, d), lambda i, j: (i, j, 0))
    return pl.pallas_call(
        functools.partial(_mlp_body, final_norm),
        grid=(b, l // tm),
        in_specs=[tok, vec, vec, vec, _const_spec((1, d)), _const_spec(w1.shape), _const_spec(w2.shape),
                  _const_spec((1, d))],
        out_specs=tok,
        out_shape=jax.ShapeDtypeStruct(x.shape, F32),
        compiler_params=_params(("parallel", "parallel")),
        name="sqrelu_mlp",
    )(x, shift, scale, gate, g, w1, w2, g_final)


def _block_diag(w):
    h, bs, _ = w.shape
    eye = jnp.eye(h, dtype=w.dtype)
    return (eye[:, None, :, None] * w[:, :, None, :]).reshape(h * bs, h * bs)


def _lru_gate_weights(w_a, w_x):
    h, bs, _ = w_a.shape
    per = max(V7X_MXU_DIM // bs, 1)
    groups = [jnp.concatenate([_block_diag(w_a[g:g + per]), _block_diag(w_x[g:g + per])], axis=1)
              for g in range(0, h, per)]
    return (0.5 * jnp.stack(groups)).astype(BF16)


def kernel(x, c, ctx, c_ctx, w_mod, b_mod, g_norm1, g_norm2, w_in, lru_conv_w, lru_conv_b, lru_w_a, lru_b_a, lru_w_x, lru_b_x, lru_lam, pool_w, pool_scale, sconv_w, w_br_lru, w_br_fourier, w_br_pool, w_br_sconv, w_out, w_ff1, w_ff2, g_final):
    bn, seq, d = x.shape
    depth = w_mod.shape[0]
    d_lru = lru_conv_w.shape[2]
    widths = (d_lru, w_br_fourier.shape[1], w_br_pool.shape[1], w_br_sconv.shape[1])
    n_small = 2 * widths[0] + widths[1] + widths[2] + 3 * widths[3]
    assert bn % LRU_BATCH == 0 and w_in.shape[2] == n_small + N_BRANCH * d

    pad = (-(bn + 1)) % V7X_SUBLANES
    c_all = jnp.concatenate([c, c_ctx[None, :], jnp.zeros((pad, d), F32)], axis=0)
    mod_all = _mod_call(c_all, w_mod, b_mod).reshape(depth, bn + 1 + pad, N_MOD, 1, d)

    row = lambda v: v.reshape(1, -1)
    g_fin = row(g_final)
    zeros_state = jnp.zeros((bn, d_lru), F32)

    for l in range(depth):
        mx = [mod_all[l, :bn, i] for i in range(N_MOD)]
        mc = [jnp.broadcast_to(mod_all[l, bn:bn + 1, i], (bn, 1, d)) for i in range(N_MOD)]
        w_small = w_in[l][:, :n_small].astype(BF16)
        w_gate = (0.5 * w_in[l][:, n_small:]).astype(BF16)
        wd = [_lru_gate_weights(lru_w_a[l, dr], lru_w_x[l, dr]) for dr in range(2)]
        bd = [0.5 * jnp.concatenate([lru_b_a[l, dr], lru_b_x[l, dr]])[None, :] for dr in range(2)]
        lam = [row(lru_lam[l, dr]) for dr in range(2)]
        pw = _block_diag(pool_w[l]).astype(BF16)
        branch_w = [w.astype(BF16) for w in (w_br_lru[l], w_br_fourier[l], w_br_pool[l], w_br_sconv[l])]
        wo = w_out[l].astype(BF16)
        w1 = w_ff1[l].astype(BF16)
        w2 = w_ff2[l].astype(BF16)
        g1, g2 = row(g_norm1[l]), row(g_norm2[l])
        cw, cb = lru_conv_w[l], row(lru_conv_b[l])
        last = l == depth - 1

        def mixer(s, m, row_len, h0):
            xa, xb, gz, zf, zp, gb, v = _inproj_call(s, m[0], m[1], g1, w_small, cw, cb, widths)
            hf, hf_last = _lru_call(0, xa, xb, wd[0], bd[0], lam[0], h0[0])
            y_lru, hb_last = _lru_call(1, xa, xb, wd[1], bd[1], lam[1], h0[1], hf=hf, gz=gz)
            y_f = _fourier_call(zf)
            y_p, y_s = _poolconv_call(zp, gb, v, row_len, pw, row(pool_scale[l]), sconv_w[l])
            s = _merge_call(s, y_lru, y_f, y_p, y_s, m[0], m[1], m[2], g1, w_gate, *branch_w, wo)
            return s, (hf_last, hb_last)

        if not last:
            ctx, h_ctx = mixer(ctx, mc, ctx.shape[1], (zeros_state, zeros_state))
            ctx = _mlp_call(ctx, mc[3], mc[4], mc[5], g2, w1, w2, g_fin, False)
        else:
            xa_c, xb_c = _inproj_call(ctx, mc[0], mc[1], g1, w_small[:, :d_lru], cw, cb)
            h_ctx = tuple(_lru_call(dr, xa_c, xb_c, wd[dr], bd[dr], lam[dr], zeros_state, emit=False)[0]
                          for dr in range(2))
        x, _ = mixer(x, mx, GRID_W, h_ctx)
        x = _mlp_call(x, mx[3], mx[4], mx[5], g2, w1, w2, g_fin, last)
    return x
```

```python
import functools

import numpy as np
import jax
import jax.numpy as jnp
from jax import lax
from jax.experimental import pallas as pl
from jax.experimental.pallas import tpu as pltpu

F32 = jnp.float32
BF16 = jnp.bfloat16

EPS = 1e-6
LRU_C = 8.0
LRU_CONV = 4
LRU_CONV_LEFT = 2
POOL_WINDOWS = (2, 4, 8, 16)
N_BRANCH = 4
N_MOD = 6
GRID_W = 64
LOG2_E = 1.4426950408889634
SQRT_GUARD = 1e-30

V7X_LANES = 128
V7X_SUBLANES = 8
V7X_MXU_DIM = 256
V7X_VMEM_LIMIT_BYTES = 56 * 1024 * 1024

INPROJ_TILE = 1024
MERGE_TILE = 1024
MLP_TILE = 1024
POOL_TILE = 256
POOL_UNROLL = 4
LRU_TILE = 256
LRU_BATCH = V7X_SUBLANES
LRU_PREP_UNROLL = 8
SCAN_UNROLL = 8
DFT_RADIX = 4


def _params(sem, vmem=V7X_VMEM_LIMIT_BYTES):
    return pltpu.CompilerParams(dimension_semantics=sem, vmem_limit_bytes=vmem)


def _const_spec(shape):
    nd = len(shape)
    return pl.BlockSpec(shape, lambda *_: (0,) * nd, pipeline_mode=pl.Buffered(1))


def _norm_mod(x, g, shift, scale):
    ms = jnp.mean(x * x, axis=-1, keepdims=True)
    y = (x * lax.rsqrt(ms + EPS)) * g
    return y * (1.0 + scale) + shift


def _dot(a, b):
    return jnp.dot(a, b, preferred_element_type=F32)


def _mod_body(c_ref, w_ref, b_ref, o_ref):
    c = c_ref[...]
    s = c * jax.nn.sigmoid(c)
    o_ref[0] = _dot(s.astype(BF16), w_ref[0].astype(BF16)) + b_ref[0]


def _mod_call(c_all, w_mod, b_mod):
    depth, d, n = w_mod.shape
    rows = c_all.shape[0]
    tn = n // 4
    return pl.pallas_call(
        _mod_body,
        grid=(depth, n // tn),
        in_specs=[
            pl.BlockSpec((rows, d), lambda l, j: (0, 0)),
            pl.BlockSpec((1, d, tn), lambda l, j: (l, 0, j)),
            pl.BlockSpec((1, 1, tn), lambda l, j: (l, 0, j)),
        ],
        out_specs=pl.BlockSpec((1, rows, tn), lambda l, j: (l, 0, j)),
        out_shape=jax.ShapeDtypeStruct((depth, rows, n), F32),
        compiler_params=_params(("parallel", "parallel")),
        name="adaln_mod",
    )(c_all, w_mod, b_mod.reshape(depth, 1, n))


def _shift_rows(z, prev, nxt, k, row8):
    tl = z.shape[0]
    if k < 0:
        r = pltpu.roll(z, -k, axis=0)
        head = jnp.where(row8 < -k, pltpu.roll(prev, -k, axis=0), r[:V7X_SUBLANES])
        return jnp.concatenate([head, r[V7X_SUBLANES:]], axis=0)
    r = pltpu.roll(z, tl - k, axis=0)
    tail = jnp.where(row8 >= V7X_SUBLANES - k, pltpu.roll(nxt, V7X_SUBLANES - k, axis=0), r[tl - V7X_SUBLANES:])
    return jnp.concatenate([r[:tl - V7X_SUBLANES], tail], axis=0)


def _inproj_body(nt, lru_only, x_ref, xp_ref, xn_ref, sh_ref, sc_ref, g_ref, w_ref, cw_ref, cb_ref, xa_ref, xb_ref, *o_refs):
    j = pl.program_id(1)
    tl = x_ref.shape[1]
    dl = xa_ref.shape[2]
    g, sh, sc = g_ref[...], sh_ref[0], sc_ref[0]
    rows = jnp.concatenate([x_ref[0], xp_ref[0], xn_ref[0]], axis=0)
    u_all = _norm_mod(rows, g, sh, sc).astype(BF16)
    z_all = _dot(u_all, w_ref[:, 0:dl])
    z = z_all[:tl]
    prev = jnp.where(j > 0, z_all[tl:tl + V7X_SUBLANES], 0.0)
    nxt = jnp.where(j < nt - 1, z_all[tl + V7X_SUBLANES:], 0.0)
    row8 = lax.broadcasted_iota(jnp.int32, (V7X_SUBLANES, dl), 0)
    xa = cb_ref[...] + cw_ref[LRU_CONV_LEFT:LRU_CONV_LEFT + 1, :] * z
    for tap in range(LRU_CONV):
        k = tap - LRU_CONV_LEFT
        if k != 0:
            xa = xa + cw_ref[tap:tap + 1, :] * _shift_rows(z, prev, nxt, k, row8)
    xa_ref[0] = xa
    xb_ref[0] = xa.astype(BF16)
    if lru_only:
        return
    gz_ref, zf_ref, zp_ref, gb_ref, v_ref = o_refs
    u = u_all[:tl]
    off = dl
    gz_ref[0] = jax.nn.gelu(_dot(u, w_ref[:, off:off + dl]), approximate=True)
    off += dl
    for o_ref in (zf_ref, zp_ref):
        wd = o_ref.shape[2]
        o_ref[0] = _dot(u, w_ref[:, off:off + wd]).astype(o_ref.dtype)
        off += wd
    ds = gb_ref.shape[2]
    zs = _dot(u, w_ref[:, off:off + 3 * ds])
    gb_ref[0] = zs[:, 0:ds]
    v_ref[0] = zs[:, ds:2 * ds] * zs[:, 2 * ds:3 * ds]


def _inproj_call(x, shift, scale, g, w, conv_w, conv_b, widths=None):
    b, l, d = x.shape
    tl = min(l, INPROJ_TILE)
    nt = l // tl
    dl = conv_w.shape[1]
    tiles8 = tl // V7X_SUBLANES
    vec = pl.BlockSpec((1, 1, d), lambda i, j: (i, 0, 0))
    prev = pl.BlockSpec((1, V7X_SUBLANES, d), lambda i, j: (i, jnp.maximum(j * tiles8 - 1, 0), 0))
    nxt = pl.BlockSpec((1, V7X_SUBLANES, d), lambda i, j: (i, jnp.minimum((j + 1) * tiles8, l // V7X_SUBLANES - 1), 0))
    outs = [(dl, F32), (dl, BF16)]
    if widths is not None:
        _, cf, dp, ds = widths
        outs += [(dl, F32), (cf, BF16), (dp, F32), (ds, F32), (ds, F32)]
    return pl.pallas_call(
        functools.partial(_inproj_body, nt, widths is None),
        grid=(b, nt),
        in_specs=[
            pl.BlockSpec((1, tl, d), lambda i, j: (i, j, 0)), prev, nxt,
            vec, vec,
            _const_spec((1, d)),
            _const_spec(w.shape), _const_spec(conv_w.shape), _const_spec((1, dl)),
        ],
        out_specs=[pl.BlockSpec((1, tl, wd), lambda i, j: (i, j, 0)) for wd, _ in outs],
        out_shape=[jax.ShapeDtypeStruct((b, l, wd), dt) for wd, dt in outs],
        compiler_params=_params(("parallel", "parallel")),
        name="in_proj",
    )(x, x, x, shift, scale, g, w, conv_w, conv_b)


def _lru_slab_pitch(tl):
    return tl * LRU_BATCH + V7X_SUBLANES


def _lru_body(direction, emit, *refs):
    if not emit:
        (xa_ref, xb_ref, wd_ref, bd_ref, lam_ref, h0_ref, hl_ref, ab_s, h_s) = refs
    elif direction == 0:
        (xa_ref, xb_ref, wd_ref, bd_ref, lam_ref, h0_ref, out_ref, hl_ref, ab_s, h_s) = refs
    else:
        (xa_ref, xb_ref, wd_ref, bd_ref, lam_ref, h0_ref, hf_ref, out_ref, hl_ref, ab_s, h_s) = refs
    nb, tl, dl = xa_ref.shape
    nslab = dl // V7X_LANES
    pitch = _lru_slab_pitch(tl)
    h_to_out = emit and direction == 0

    def slab(kind, s, start, size, stride=None):
        return pl.ds((2 * s + kind) * pitch + start, size, stride=stride)

    @pl.when(pl.program_id(1) == 0)
    def _():
        h_s[...] = h0_ref[...]

    half_rate = (-0.5 * LRU_C * LOG2_E) * jax.nn.softplus(-lam_ref[...])
    ngroups = wd_ref.shape[0]
    gw = dl // ngroups

    def prep(bi, carry):
        xa = xa_ref[bi]
        xb = xb_ref[bi]
        parts = [_dot(xb[:, g * gw:(g + 1) * gw], wd_ref[g]) for g in range(ngroups)]
        ta = jnp.tanh(jnp.concatenate([p[:, :gw] for p in parts], axis=1) + bd_ref[:, :dl])
        tx = jnp.tanh(jnp.concatenate([p[:, gw:] for p in parts], axis=1) + bd_ref[:, dl:])
        a = jnp.exp2(half_rate * ta + half_rate)
        y = 1.0 - a * a
        hx = 0.5 * xa
        bb = (y * lax.rsqrt(jnp.maximum(y, SQRT_GUARD))) * (hx * tx + hx)
        for s in range(nslab):
            lanes = slice(s * V7X_LANES, (s + 1) * V7X_LANES)
            ab_s[slab(0, s, bi, tl, LRU_BATCH), :] = a[:, lanes]
            ab_s[slab(1, s, bi, tl, LRU_BATCH), :] = bb[:, lanes]
        return carry

    lax.fori_loop(0, nb, prep, 0, unroll=LRU_PREP_UNROLL)

    def steps(g, hs):
        first = g * SCAN_UNROLL if direction == 0 else tl - SCAN_UNROLL - g * SCAN_UNROLL
        base = pl.multiple_of(first * LRU_BATCH, SCAN_UNROLL * LRU_BATCH)
        for k in range(SCAN_UNROLL):
            t = k if direction == 0 else SCAN_UNROLL - 1 - k
            row = base + t * LRU_BATCH
            new = []
            for s in range(nslab):
                h = ab_s[slab(0, s, row, LRU_BATCH), :] * hs[s] + ab_s[slab(1, s, row, LRU_BATCH), :]
                if h_to_out:
                    out_ref[s, pl.ds(row, LRU_BATCH), :] = h
                elif emit:
                    ab_s[slab(1, s, row, LRU_BATCH), :] = h + hf_ref[s, pl.ds(row, LRU_BATCH), :]
                new.append(h)
            hs = tuple(new)
        return hs

    h_init = tuple(h_s[:, s * V7X_LANES:(s + 1) * V7X_LANES] for s in range(nslab))
    h_fin = lax.fori_loop(0, tl // SCAN_UNROLL, steps, h_init)
    h_last = jnp.concatenate(h_fin, axis=-1)
    h_s[...] = h_last
    hl_ref[...] = h_last

    if emit and direction == 1:
        def write(bi, carry):
            out_ref[bi] = jnp.concatenate([ab_s[slab(1, s, bi, tl, LRU_BATCH), :] for s in range(nslab)], axis=-1)
            return carry

        lax.fori_loop(0, nb, write, 0)


def _lru_call(direction, xa, xb, wd, bd, lam, h0, hf=None, emit=True):
    b, l, dl = xa.shape
    tl = min(l, LRU_TILE)
    nt = l // tl
    nb = LRU_BATCH
    nslab = dl // V7X_LANES
    tile = (lambda g, j: (g, j, 0)) if direction == 0 else (lambda g, j: (g, nt - 1 - j, 0))
    main = pl.BlockSpec((nb, tl, dl), tile)
    slab_block = (None, None, nslab, tl * nb, V7X_LANES)
    state = pl.BlockSpec((nb, dl), lambda g, j: (g, 0))
    in_specs = [main, main, _const_spec(wd.shape), _const_spec((1, 2 * dl)), _const_spec((1, dl)), state]
    args = [xa, xb, wd, bd, lam, h0]
    state_shape = jax.ShapeDtypeStruct((b, dl), F32)
    out_specs, out_shape = [state], [state_shape]
    if emit and direction == 0:
        out_specs = [pl.BlockSpec(slab_block, lambda g, j: (g, j, 0, 0, 0)), state]
        out_shape = [jax.ShapeDtypeStruct((b // nb, nt, nslab, tl * nb, V7X_LANES), F32), state_shape]
    elif emit:
        in_specs += [pl.BlockSpec(slab_block, lambda g, j: (g, nt - 1 - j, 0, 0, 0))]
        args += [hf]
        out_specs, out_shape = [main, state], [jax.ShapeDtypeStruct((b, l, dl), F32), state_shape]
    slabs = pltpu.VMEM((2 * nslab * _lru_slab_pitch(tl), V7X_LANES), F32)
    return pl.pallas_call(
        functools.partial(_lru_body, direction, emit),
        grid=(b // nb, nt),
        in_specs=in_specs,
        out_specs=out_specs,
        out_shape=out_shape,
        scratch_shapes=[slabs, pltpu.VMEM((nb, dl), F32)],
        compiler_params=_params(("parallel", "arbitrary")),
        name=f"rglru_dir{direction}",
    )(*args)


def _fourier_body(scale, zf_ref, wc_ref, tab_ref, o_ref, y_s):
    l, cf = zf_ref.shape[1], zf_ref.shape[2]
    lq = l // DFT_RADIX
    ab = _dot(zf_ref[0], wc_ref[...])
    a0, a1, a2, a3 = (ab[q * lq:(q + 1) * lq, :cf] for q in range(DFT_RADIX))
    b0, b1, b2, b3 = (ab[q * lq:(q + 1) * lq, cf:] for q in range(DFT_RADIX))
    g_re = (a0 + a1 + a2 + a3, a0 - b1 - a2 + b3, a0 - a1 + a2 - a3, a0 + b1 - a2 - b3)
    g_im = (-(b0 + b1 + b2 + b3), a3 + b2 - a1 - b0, b1 + b3 - b0 - b2, a1 + b2 - a3 - b0)
    for r in range(DFT_RADIX):
        g = jnp.concatenate([g_re[r], g_im[r]], axis=0).astype(BF16)
        yr = _dot(tab_ref[r], g) * scale
        for s in range(cf // V7X_LANES):
            y_s[s, pl.ds(r, lq, stride=DFT_RADIX), :] = yr[:, s * V7X_LANES:(s + 1) * V7X_LANES]
    o_ref[0] = jnp.concatenate([y_s[s] for s in range(cf // V7X_LANES)], axis=-1).astype(o_ref.dtype)


def _dft_tables(l, cf):
    lq = l // DFT_RADIX
    m = np.arange(lq)
    tabs = []
    for r in range(DFT_RADIX):
        k = DFT_RADIX * np.arange(lq) + r
        ang = 2.0 * np.pi * ((k[:, None] * m[None, :]) % l) / l
        tabs.append(np.concatenate([np.cos(ang), np.sin(ang)], axis=1))
    c = np.arange(cf)
    ang_c = 2.0 * np.pi * ((c[:, None] * c[None, :]) % cf) / cf
    wc = np.concatenate([np.cos(ang_c), np.sin(ang_c)], axis=1)
    return jnp.asarray(np.stack(tabs).astype(np.float32)).astype(BF16), jnp.asarray(wc.astype(np.float32)).astype(BF16)


def _fourier_call(zf):
    b, l, cf = zf.shape
    tabs, wc = _dft_tables(l, cf)
    blk = pl.BlockSpec((1, l, cf), lambda i: (i, 0, 0))
    return pl.pallas_call(
        functools.partial(_fourier_body, float(1.0 / np.sqrt(l * cf))),
        grid=(b,),
        in_specs=[blk, _const_spec(wc.shape), _const_spec(tabs.shape)],
        out_specs=blk,
        out_shape=jax.ShapeDtypeStruct((b, l, cf), BF16),
        scratch_shapes=[pltpu.VMEM((cf // V7X_LANES, l, V7X_LANES), F32)],
        compiler_params=_params(("parallel",)),
        name="fourier_mix",
    )(zf, wc, tabs)


def _poolconv_body(zp_ref, gb_ref, v_ref, pm_ref, ic_ref, pw_ref, ps_ref, sw_ref, yp_ref, ys_ref):
    l, dp = zp_ref.shape[1], zp_ref.shape[2]
    gw = dp // pm_ref.shape[0]
    per_slab = V7X_LANES // gw
    lane_group = lax.broadcasted_iota(jnp.int32, (POOL_TILE, V7X_LANES), 1) // gw

    def pool_tile(i, carry):
        rows = pl.ds(pl.multiple_of(i * POOL_TILE, POOL_TILE), POOL_TILE)
        u = zp_ref[0, rows, :]
        hi = u.astype(BF16)
        lo = (u - hi.astype(F32)).astype(BF16)
        slabs = []
        for s in range(dp // V7X_LANES):
            lanes = slice(s * V7X_LANES, (s + 1) * V7X_LANES)
            both = jnp.concatenate([hi[:, lanes], lo[:, lanes]], axis=1)
            acc = None
            for k in range(per_slab):
                rr = _dot(pm_ref[s * per_slab + k], both)
                sg = rr[:, :V7X_LANES] + rr[:, V7X_LANES:]
                acc = sg if acc is None else jnp.where(lane_group == k, sg, acc)
            slabs.append(acc)
        p = jnp.concatenate(slabs, axis=1) * ic_ref[...] - u
        yp_ref[0, rows, :] = (_dot(p.astype(BF16), pw_ref[...]) * ps_ref[...]).astype(yp_ref.dtype)
        return carry

    lax.fori_loop(0, l // POOL_TILE, pool_tile, 0, unroll=min(POOL_UNROLL, l // POOL_TILE))

    v = v_ref[0]
    row8 = lax.broadcasted_iota(jnp.int32, (V7X_SUBLANES, v.shape[1]), 0)
    zeros8 = jnp.zeros((V7X_SUBLANES, v.shape[1]), F32)
    conv = (sw_ref[0:1, :] * _shift_rows(v, zeros8, zeros8, -1, row8) + sw_ref[1:2, :] * v
            + sw_ref[2:3, :] * _shift_rows(v, zeros8, zeros8, 1, row8))
    ys_ref[0] = (gb_ref[0] * conv).astype(ys_ref.dtype)


def _pool_tables(row_len, dp):
    t = np.arange(POOL_TILE)
    r0 = (t // row_len) * row_len
    mats, inv = [], []
    for w in POOL_WINDOWS:
        lo = np.maximum(t - w // 2, r0)
        hi = np.minimum(t + w // 2, r0 + row_len)
        mats.append(((t[None, :] >= lo[:, None]) & (t[None, :] < hi[:, None])).astype(np.float32))
        inv.append(np.repeat((1.0 / (hi - lo))[:, None], dp // len(POOL_WINDOWS), axis=1))
    pm = jnp.asarray(np.stack(mats)).astype(BF16)
    ic = jnp.asarray(np.concatenate(inv, axis=1).astype(np.float32))
    return pm, ic


def _poolconv_call(zp, gb, v, row_len, pw, pscale, sw):
    b, l, dp = zp.shape
    assert row_len <= POOL_TILE and POOL_TILE % row_len == 0 and l % POOL_TILE == 0
    assert V7X_LANES % (dp // len(POOL_WINDOWS)) == 0
    pm, ic = _pool_tables(row_len, dp)
    ds = gb.shape[2]

    def seq(width):
        return pl.BlockSpec((1, l, width), lambda i: (i, 0, 0))

    return pl.pallas_call(
        _poolconv_body,
        grid=(b,),
        in_specs=[seq(dp), seq(ds), seq(ds), _const_spec(pm.shape), _const_spec(ic.shape), _const_spec(pw.shape),
                  _const_spec((1, dp)), _const_spec(sw.shape)],
        out_specs=[seq(dp), seq(ds)],
        out_shape=[jax.ShapeDtypeStruct((b, l, dp), BF16), jax.ShapeDtypeStruct((b, l, ds), BF16)],
        compiler_params=_params(("parallel",)),
        name="pool_sconv",
    )(zp, gb, v, pm, ic, pw, pscale, sw)


def _merge_body(x_ref, hs_ref, gz_ref, yf_ref, yp_ref, ys_ref, sh_ref, sc_ref, gt_ref, g_ref,
                wg_ref, wl_ref, wf_ref, wp_ref, ws_ref, wo_ref, o_ref):
    x = x_ref[0]
    d = x.shape[-1]
    u = _norm_mod(x, g_ref[...], sh_ref[0], sc_ref[0]).astype(BF16)
    y_lru = (hs_ref[0] * gz_ref[0]).astype(BF16)
    branches = (y_lru, yf_ref[0], yp_ref[0], ys_ref[0])
    twice = None
    for i, (y, w_ref) in enumerate(zip(branches, (wl_ref, wf_ref, wp_ref, ws_ref))):
        t = jnp.tanh(_dot(u, wg_ref[:, i * d:(i + 1) * d]))
        p = _dot(y, w_ref[...])
        term = t * p + p
        twice = term if twice is None else twice + term
    o_ref[0] = x + gt_ref[0] * _dot((0.5 * twice).astype(BF16), wo_ref[...])


def _merge_call(x, hs, gz, yf, yp, ys, shift, scale, gate, g, wg, wl, wf, wp, ws, wo):
    b, l, d = x.shape
    tm = min(l, MERGE_TILE)
    vec = pl.BlockSpec((1, 1, d), lambda i, j: (i, 0, 0))

    def tok(a):
        return pl.BlockSpec((1, tm, a.shape[2]), lambda i, j: (i, j, 0))

    return pl.pallas_call(
        _merge_body,
        grid=(b, l // tm),
        in_specs=[tok(x), tok(hs), tok(gz), tok(yf), tok(yp), tok(ys), vec, vec, vec, _const_spec((1, d))]
        + [_const_spec(w.shape) for w in (wg, wl, wf, wp, ws, wo)],
        out_specs=tok(x),
        out_shape=jax.ShapeDtypeStruct(x.shape, F32),
        compiler_params=_params(("parallel", "parallel")),
        name="merge_out",
    )(x, hs, gz, yf, yp, ys, shift, scale, gate, g, wg, wl, wf, wp, ws, wo)


def _mlp_body(final_norm, x_ref, sh_ref, sc_ref, gt_ref, g_ref, w1_ref, w2_ref, gf_ref, o_ref):
    x = x_ref[0]
    d = x.shape[-1]
    dff = w1_ref.shape[1]
    u = _norm_mod(x, g_ref[...], sh_ref[0], sc_ref[0]).astype(BF16)
    acc = None
    for k in range(dff // d):
        cols = slice(k * d, (k + 1) * d)
        h = jnp.square(jnp.maximum(_dot(u, w1_ref[:, cols]), 0.0)).astype(BF16)
        part = _dot(h, w2_ref[cols, :])
        acc = part if acc is None else acc + part
    y = x + gt_ref[0] * acc
    if final_norm:
        ms = jnp.mean(y * y, axis=-1, keepdims=True)
        y = (y * lax.rsqrt(ms + EPS)) * gf_ref[...]
    o_ref[0] = y


def _mlp_call(x, shift, scale, gate, g, w1, w2, g_final, final_norm):
    b, l, d = x.shape
    tm = min(l, MLP_TILE)
    vec = pl.BlockSpec((1, 1, d), lambda i, j: (i, 0, 0))
    tok = pl.BlockSpec((1, tm, d), lambda i, j: (i, j, 0))
    return pl.pallas_call(
        functools.partial(_mlp_body, final_norm),
        grid=(b, l // tm),
        in_specs=[tok, vec, vec, vec, _const_spec((1, d)), _const_spec(w1.shape), _const_spec(w2.shape),
                  _const_spec((1, d))],
        out_specs=tok,
        out_shape=jax.ShapeDtypeStruct(x.shape, F32),
        compiler_params=_params(("parallel", "parallel")),
        name="sqrelu_mlp",
    )(x, shift, scale, gate, g, w1, w2, g_final)


def _block_diag(w):
    h, bs, _ = w.shape
    eye = jnp.eye(h, dtype=w.dtype)
    return (eye[:, None, :, None] * w[:, :, None, :]).reshape(h * bs, h * bs)


def _lru_gate_weights(w_a, w_x):
    h, bs, _ = w_a.shape
    per = max(V7X_MXU_DIM // bs, 1)
    groups = [jnp.concatenate([_block_diag(w_a[g:g + per]), _block_diag(w_x[g:g + per])], axis=1)
              for g in range(0, h, per)]
    return (0.5 * jnp.stack(groups)).astype(BF16)


def kernel(x, c, ctx, c_ctx, w_mod, b_mod, g_norm1, g_norm2, w_in, lru_conv_w, lru_conv_b, lru_w_a, lru_b_a, lru_w_x, lru_b_x, lru_lam, pool_w, pool_scale, sconv_w, w_br_lru, w_br_fourier, w_br_pool, w_br_sconv, w_out, w_ff1, w_ff2, g_final):
    bn, seq, d = x.shape
    depth = w_mod.shape[0]
    d_lru = lru_conv_w.shape[2]
    widths = (d_lru, w_br_fourier.shape[1], w_br_pool.shape[1], w_br_sconv.shape[1])
    n_small = 2 * widths[0] + widths[1] + widths[2] + 3 * widths[3]
    assert bn % LRU_BATCH == 0 and w_in.shape[2] == n_small + N_BRANCH * d

    pad = (-(bn + 1)) % V7X_SUBLANES
    c_all = jnp.concatenate([c, c_ctx[None, :], jnp.zeros((pad, d), F32)], axis=0)
    mod_all = _mod_call(c_all, w_mod, b_mod).reshape(depth, bn + 1 + pad, N_MOD, 1, d)

    row = lambda v: v.reshape(1, -1)
    g_fin = row(g_final)
    zeros_state = jnp.zeros((bn, d_lru), F32)

    for l in range(depth):
        mx = [mod_all[l, :bn, i] for i in range(N_MOD)]
        mc = [jnp.broadcast_to(mod_all[l, bn:bn + 1, i], (bn, 1, d)) for i in range(N_MOD)]
        w_small = w_in[l][:, :n_small].astype(BF16)
        w_gate = (0.5 * w_in[l][:, n_small:]).astype(BF16)
        wd = [_lru_gate_weights(lru_w_a[l, dr], lru_w_x[l, dr]) for dr in range(2)]
        bd = [0.5 * jnp.concatenate([lru_b_a[l, dr], lru_b_x[l, dr]])[None, :] for dr in range(2)]
        lam = [row(lru_lam[l, dr]) for dr in range(2)]
        pw = _block_diag(pool_w[l]).astype(BF16)
        branch_w = [w.astype(BF16) for w in (w_br_lru[l], w_br_fourier[l], w_br_pool[l], w_br_sconv[l])]
        wo = w_out[l].astype(BF16)
        w1 = w_ff1[l].astype(BF16)
        w2 = w_ff2[l].astype(BF16)
        g1, g2 = row(g_norm1[l]), row(g_norm2[l])
        cw, cb = lru_conv_w[l], row(lru_conv_b[l])
        last = l == depth - 1

        def mixer(s, m, row_len, h0):
            xa, xb, gz, zf, zp, gb, v = _inproj_call(s, m[0], m[1], g1, w_small, cw, cb, widths)
            hf, hf_last = _lru_call(0, xa, xb, wd[0], bd[0], lam[0], h0[0])
            h_sum, hb_last = _lru_call(1, xa, xb, wd[1], bd[1], lam[1], h0[1], hf=hf)
            y_f = _fourier_call(zf)
            y_p, y_s = _poolconv_call(zp, gb, v, row_len, pw, row(pool_scale[l]), sconv_w[l])
            s = _merge_call(s, h_sum, gz, y_f, y_p, y_s, m[0], m[1], m[2], g1, w_gate, *branch_w, wo)
            return s, (hf_last, hb_last)

        if not last:
            ctx, h_ctx = mixer(ctx, mc, ctx.shape[1], (zeros_state, zeros_state))
            ctx = _mlp_call(ctx, mc[3], mc[4], mc[5], g2, w1, w2, g_fin, False)
        else:
            xa_c, xb_c = _inproj_call(ctx, mc[0], mc[1], g1, w_small[:, :d_lru], cw, cb)
            h_ctx = tuple(_lru_call(dr, xa_c, xb_c, wd[dr], bd[dr], lam[dr], zeros_state, emit=False)[0]
                          for dr in range(2))
        x, _ = mixer(x, mx, GRID_W, h_ctx)
        x = _mlp_call(x, mx[3], mx[4], mx[5], g2, w1, w2, g_fin, last)
    return x
```
